```python
import jax, jax.numpy as jnp
from jax import lax
import numpy as np

D_MODEL = 4096
BATCH = 2
SEQ = 8192
DEPTH = 1

GRID_W = 64
CTX_LEN = 256
RET_HEADS = 8
RET_HD = 256
RET_W = RET_HEADS * RET_HD
RET_CHUNK = 128
ROPE_BASE = 10000.0
LRU_W = 2048
LRU_HEADS = 8
LRU_HD = LRU_W // LRU_HEADS
LRU_C = 8.0
CONV_W = 4
CONV_LEFT = 2
N_EXPERTS = 16
EXPERT_FF = 2048
EC_CAPACITY = 2
EPS = 1e-6
Q_OFF = 0
K_OFF = Q_OFF + RET_W
V_OFF = K_OFF + RET_W
G_OFF = V_OFF + RET_W
LX_OFF = G_OFF + RET_W
LY_OFF = LX_OFF + LRU_W
GR_OFF = LY_OFF + LRU_W
GL_OFF = GR_OFF + D_MODEL
IN_W = GL_OFF + D_MODEL

kernel_name = "hybrid_retention_rglru_ec_moe_dit"


def rmsnorm(x, g):
    xf = x.astype(jnp.float32)
    y = xf * lax.rsqrt(jnp.mean(xf * xf, axis=-1, keepdims=True) + EPS)
    return (y * g.astype(jnp.float32)).astype(x.dtype)


def cols(t, w, off, width):
    return t @ w[:, off:off + width]


def heads(t):
    b, n, _ = t.shape
    return t.reshape(b, n, RET_HEADS, RET_HD).transpose(0, 2, 1, 3)


def rotary_2d(t, row, col):
    quarter = RET_HD // 4
    half = RET_HD // 2
    freqs = ROPE_BASE ** (-jnp.arange(quarter, dtype=jnp.float32) / quarter)

    def rot(u, pos):
        ang = pos[:, None] * freqs[None, :]
        cos, sin = jnp.cos(ang), jnp.sin(ang)
        u1 = u[..., :quarter].astype(jnp.float32)
        u2 = u[..., quarter:].astype(jnp.float32)
        return jnp.concatenate([u1 * cos - u2 * sin, u1 * sin + u2 * cos], axis=-1)

    out = jnp.concatenate([rot(t[..., :half], row), rot(t[..., half:], col)], axis=-1)
    return out.astype(t.dtype)


def retention_scan(q, k, v, log_gamma, state0):
    b, h, n, dk = q.shape
    dv = v.shape[-1]
    nc = n // RET_CHUNK
    lg = log_gamma.astype(jnp.float32)
    j = jnp.arange(RET_CHUNK, dtype=jnp.float32)
    diff = j[:, None] - j[None, :]
    intra = jnp.where(diff[None] >= 0, jnp.exp(jnp.maximum(diff, 0.0)[None] * lg[:, None, None]), 0.0)
    q_dec = jnp.exp((j + 1.0)[None, :] * lg[:, None])[None, :, :, None]
    k_dec = jnp.exp((RET_CHUNK - 1.0 - j)[None, :] * lg[:, None])[None, :, :, None]
    chunk_dec = jnp.exp(RET_CHUNK * lg)[None, :, None, None]

    def to_chunks(t):
        return t.astype(jnp.float32).reshape(b, h, nc, RET_CHUNK, t.shape[-1]).transpose(2, 0, 1, 3, 4)

    def step(state, qkv):
        qi, ki, vi = qkv
        s = jnp.einsum('bhid,bhjd->bhij', qi, ki) * intra
        o = jnp.einsum('bhij,bhjv->bhiv', s, vi) + jnp.einsum('bhid,bhdv->bhiv', qi, state) * q_dec
        state = state * chunk_dec + jnp.einsum('bhjd,bhjv->bhdv', ki * k_dec, vi)
        return state, o

    state, o = lax.scan(step, state0.astype(jnp.float32), (to_chunks(q), to_chunks(k), to_chunks(v)))
    return o.transpose(1, 2, 0, 3, 4).reshape(b, h, n, dv), state


def retention_bidir(q, k, v, log_gamma2, s0f, s0b):
    of, sf = retention_scan(q, k, v, log_gamma2[0], s0f)
    ob, sb = retention_scan(jnp.flip(q, 2), jnp.flip(k, 2), jnp.flip(v, 2), log_gamma2[1], s0b)
    return of + jnp.flip(ob, 2), sf, sb


def retention_final_state(k, v, log_gamma):
    n = k.shape[2]
    w = jnp.exp((n - 1.0 - jnp.arange(n, dtype=jnp.float32))[None, :] * log_gamma.astype(jnp.float32)[:, None])
    return jnp.einsum('bhld,bhlv,hl->bhdv', k.astype(jnp.float32), v.astype(jnp.float32), w)


def group_rms(o):
    return o * lax.rsqrt(jnp.mean(o * o, axis=-1, keepdims=True) + EPS)


def depthwise_conv(u, w, bias):
    n = u.shape[1]
    up = jnp.pad(u, ((0, 0), (CONV_LEFT, CONV_W - 1 - CONV_LEFT), (0, 0)))
    return sum(up[:, i:i + n] * w[i] for i in range(CONV_W)) + bias


def rglru_coeffs(u, wa, ba, wx, bx, lam):
    b, n, _ = u.shape
    uf = u.astype(jnp.float32)
    uh = uf.reshape(b, n, LRU_HEADS, LRU_HD)
    r = jax.nn.sigmoid(jnp.einsum('bnhi,hij->bnhj', uh, wa).reshape(b, n, LRU_W) + ba)
    i = jax.nn.sigmoid(jnp.einsum('bnhi,hij->bnhj', uh, wx).reshape(b, n, LRU_W) + bx)
    log_a = -LRU_C * r * jax.nn.softplus(-lam.astype(jnp.float32))
    inp = jnp.sqrt(-jnp.expm1(2.0 * log_a)) * (i * uf)
    return log_a, inp


def _lin_combine(e1, e2):
    a1, b1 = e1
    a2, b2 = e2
    return a1 * a2, a2 * b1 + b2


def linear_scan(log_a, inp, h0):
    a_cum, b_cum = lax.associative_scan(_lin_combine, (jnp.exp(log_a), inp), axis=1)
    return a_cum * h0[:, None, :] + b_cum


def lru_final_state(log_a, inp):
    tail = lax.cumsum(log_a, axis=1, reverse=True) - log_a
    return jnp.sum(jnp.exp(tail) * inp, axis=1)


def lru_bidir(u, wa, ba, wx, bx, lam, h0f, h0b):
    laf, inf_ = rglru_coeffs(u, wa[0], ba[0], wx[0], bx[0], lam[0])
    lab, inb = rglru_coeffs(jnp.flip(u, 1), wa[1], ba[1], wx[1], bx[1], lam[1])
    hf = linear_scan(laf, inf_, h0f)
    hb = jnp.flip(linear_scan(lab, inb, h0b), 1)
    return hf, hb


def lru_context_states(uc, wa, ba, wx, bx, lam):
    laf, inf_ = rglru_coeffs(uc, wa[0], ba[0], wx[0], bx[0], lam[0])
    lab, inb = rglru_coeffs(jnp.flip(uc, 1), wa[1], ba[1], wx[1], bx[1], lam[1])
    return lru_final_state(laf, inf_), lru_final_state(lab, inb)


def branch_merge(hh, ret_o, lru_h, w_in, w_ret_out, w_lru_out, w_out):
    b, n, _ = hh.shape
    ret = group_rms(ret_o).transpose(0, 2, 1, 3).reshape(b, n, RET_W).astype(hh.dtype)
    ret = (ret * jax.nn.silu(cols(hh, w_in, G_OFF, RET_W))) @ w_ret_out
    lru = (jax.nn.gelu(cols(hh, w_in, LY_OFF, LRU_W)) * lru_h.astype(hh.dtype)) @ w_lru_out
    g_ret = jax.nn.sigmoid(cols(hh, w_in, GR_OFF, D_MODEL))
    g_lru = jax.nn.sigmoid(cols(hh, w_in, GL_OFF, D_MODEL))
    return (g_ret * ret + g_lru * lru) @ w_out


def expert_choice_ffn(h, router_w, w1, w3, w2):
    b, n, d = h.shape
    cap = EC_CAPACITY * n // N_EXPERTS
    aff = jax.nn.softmax(jnp.einsum('bnd,de->ben', h, router_w).astype(jnp.float32), axis=1)
    top_aff, top_idx = lax.top_k(aff, cap)
    xg = jax.vmap(lambda hb, ib: hb[ib])(h, top_idx)
    hid = jax.nn.silu(jnp.einsum('becd,edf->becf', xg, w1)) * jnp.einsum('becd,edf->becf', xg, w3)
    y = jnp.einsum('becf,efd->becd', hid, w2) * top_aff[..., None].astype(h.dtype)
    return jax.vmap(lambda yb, ib: jnp.zeros((n, d), yb.dtype).at[ib.reshape(-1)].add(yb.reshape(-1, d)))(y, top_idx)


def hybrid_layer(x, xc, c, c_ctx, p, update_ctx):
    d = D_MODEL
    mod = jax.nn.silu(c) @ p['ada_w'] + p['ada_b']
    sh1, sc1, gt1, sh2, sc2, gt2 = jnp.split(mod[:, None, :], 6, axis=-1)
    n_ctx_mod = 6 if update_ctx else 2
    modc = jax.nn.silu(c_ctx) @ p['ada_w'][:, :n_ctx_mod * d] + p['ada_b'][:n_ctx_mod * d]
    cmods = jnp.split(modc, n_ctx_mod, axis=-1)
    csh1, csc1 = cmods[0], cmods[1]

    w_in = p['w_in']
    h = rmsnorm(x, p['n_pre_mix']) * (1.0 + sc1) + sh1
    hc = rmsnorm(xc, p['n_pre_mix']) * (1.0 + csc1) + csh1

    n = x.shape[1]
    rows = n // GRID_W
    row = jnp.repeat(jnp.arange(rows, dtype=jnp.float32), GRID_W)
    col = jnp.tile(jnp.arange(GRID_W, dtype=jnp.float32), rows)
    k_scale = RET_HD ** -0.5
    lru_args = (p['gate_a_w'], p['gate_a_b'], p['gate_x_w'], p['gate_x_b'], p['lam'])

    kc = heads(cols(hc, w_in, K_OFF, RET_W)) * k_scale
    vc = heads(cols(hc, w_in, V_OFF, RET_W))
    uc = depthwise_conv(cols(hc, w_in, LX_OFF, LRU_W), p['conv_w'], p['conv_b'])
    bsz = x.shape[0]
    if update_ctx:
        qc = heads(cols(hc, w_in, Q_OFF, RET_W))
        zs = jnp.zeros((bsz, RET_HEADS, RET_HD, RET_HD), jnp.float32)
        ret_c, sf, sb = retention_bidir(qc, kc, vc, p['ret_lg'], zs, zs)
        zh = jnp.zeros((bsz, LRU_W), jnp.float32)
        hfc, hbc = lru_bidir(uc, *lru_args, zh, zh)
        h0f, h0b = hfc[:, -1], hbc[:, 0]
    else:
        sf = retention_final_state(kc, vc, p['ret_lg'][0])
        sb = retention_final_state(jnp.flip(kc, 2), jnp.flip(vc, 2), p['ret_lg'][1])
        h0f, h0b = lru_context_states(uc, *lru_args)

    q = rotary_2d(heads(cols(h, w_in, Q_OFF, RET_W)), row, col)
    k = rotary_2d(heads(cols(h, w_in, K_OFF, RET_W)), row, col) * k_scale
    v = heads(cols(h, w_in, V_OFF, RET_W))
    ret_o, _, _ = retention_bidir(q, k, v, p['ret_lg'], sf, sb)
    u = depthwise_conv(cols(h, w_in, LX_OFF, LRU_W), p['conv_w'], p['conv_b'])
    hf, hb = lru_bidir(u, *lru_args, h0f, h0b)
    mix = branch_merge(h, ret_o, hf + hb, w_in, p['w_ret_out'], p['w_lru_out'], p['w_out'])
    x = x + gt1 * rmsnorm(mix, p['n_post_mix'])

    h2 = rmsnorm(x, p['n_pre_ffn']) * (1.0 + sc2) + sh2
    moe = expert_choice_ffn(h2, p['router_w'], p['w1'], p['w3'], p['w2'])
    x = x + gt2 * rmsnorm(moe, p['n_post_ffn'])

    if update_ctx:
        cgt1, csh2, csc2, cgt2 = cmods[2], cmods[3], cmods[4], cmods[5]
        mix_c = branch_merge(hc, ret_c, hfc + hbc, w_in, p['w_ret_out'], p['w_lru_out'], p['w_out'])
        xc = xc + cgt1 * rmsnorm(mix_c, p['n_post_mix'])
        hc2 = rmsnorm(xc, p['n_pre_ffn']) * (1.0 + csc2) + csh2
        moe_c = expert_choice_ffn(hc2, p['router_w'], p['w1'], p['w3'], p['w2'])
        xc = xc + cgt2 * rmsnorm(moe_c, p['n_post_ffn'])
    return x, xc


def setup_inputs(seed: int = 0) -> dict:
    key = jax.random.key(seed)
    ks = jax.random.split(key, 32)
    d = D_MODEL

    def nrm(k, shape, s):
        return jax.random.normal(k, shape, jnp.float32) * s

    base_lg = jnp.log1p(-(2.0 ** (-5.0 - jnp.arange(RET_HEADS, dtype=jnp.float32))))
    u = jax.random.uniform(ks[19], (DEPTH, 2, LRU_W), jnp.float32, 0.9, 0.999)
    a0 = u ** (1.0 / LRU_C)
    return {
        "x": nrm(ks[0], (BATCH, SEQ, d), 1.0),
        "c": nrm(ks[1], (BATCH, d), 1.0),
        "ctx": nrm(ks[2], (BATCH, CTX_LEN, d), 1.0),
        "c_ctx": nrm(ks[3], (d,), 1.0),
        "ada_w": nrm(ks[4], (DEPTH, d, 6 * d), 0.5 * d ** -0.5),
        "ada_b": nrm(ks[5], (DEPTH, 6 * d), 0.01),
        "norm_pre_mix": 1.0 + nrm(ks[6], (DEPTH, d), 0.02),
        "norm_post_mix": 1.0 + nrm(ks[7], (DEPTH, d), 0.02),
        "norm_pre_ffn": 1.0 + nrm(ks[8], (DEPTH, d), 0.02),
        "norm_post_ffn": 1.0 + nrm(ks[9], (DEPTH, d), 0.02),
        "w_in": nrm(ks[10], (DEPTH, d, IN_W), d ** -0.5),
        "ret_log_gamma": base_lg * jnp.exp(nrm(ks[11], (DEPTH, 2, RET_HEADS), 0.1)),
        "w_ret_out": nrm(ks[12], (DEPTH, RET_W, d), RET_W ** -0.5),
        "lru_conv_w": nrm(ks[13], (DEPTH, CONV_W, LRU_W), CONV_W ** -0.5),
        "lru_conv_b": nrm(ks[14], (DEPTH, LRU_W), 0.01),
        "lru_gate_a_w": nrm(ks[15], (DEPTH, 2, LRU_HEADS, LRU_HD, LRU_HD), LRU_HD ** -0.5),
        "lru_gate_a_b": nrm(ks[16], (DEPTH, 2, LRU_W), 0.01),
        "lru_gate_x_w": nrm(ks[17], (DEPTH, 2, LRU_HEADS, LRU_HD, LRU_HD), LRU_HD ** -0.5),
        "lru_gate_x_b": nrm(ks[18], (DEPTH, 2, LRU_W), 0.01),
        "lru_lambda": jnp.log(a0) - jnp.log1p(-a0),
        "w_lru_out": nrm(ks[20], (DEPTH, LRU_W, d), LRU_W ** -0.5),
        "w_out": nrm(ks[21], (DEPTH, d, d), d ** -0.5),
        "router_w": nrm(ks[22], (DEPTH, d, N_EXPERTS), d ** -0.5),
        "expert_w1": nrm(ks[23], (DEPTH, N_EXPERTS, d, EXPERT_FF), d ** -0.5),
        "expert_w3": nrm(ks[24], (DEPTH, N_EXPERTS, d, EXPERT_FF), d ** -0.5),
        "expert_w2": nrm(ks[25], (DEPTH, N_EXPERTS, EXPERT_FF, d), EXPERT_FF ** -0.5),
    }


def reference(x, c, ctx, c_ctx, ada_w, ada_b, norm_pre_mix, norm_post_mix, norm_pre_ffn, norm_post_ffn,
              w_in, ret_log_gamma, w_ret_out, lru_conv_w, lru_conv_b, lru_gate_a_w, lru_gate_a_b,
              lru_gate_x_w, lru_gate_x_b, lru_lambda, w_lru_out, w_out, router_w, expert_w1, expert_w3,
              expert_w2):
    xc = ctx
    for l in range(DEPTH):
        p = {
            'ada_w': ada_w[l], 'ada_b': ada_b[l],
            'n_pre_mix': norm_pre_mix[l], 'n_post_mix': norm_post_mix[l],
            'n_pre_ffn': norm_pre_ffn[l], 'n_post_ffn': norm_post_ffn[l],
            'w_in': w_in[l], 'ret_lg': ret_log_gamma[l], 'w_ret_out': w_ret_out[l],
            'conv_w': lru_conv_w[l], 'conv_b': lru_conv_b[l],
            'gate_a_w': lru_gate_a_w[l], 'gate_a_b': lru_gate_a_b[l],
            'gate_x_w': lru_gate_x_w[l], 'gate_x_b': lru_gate_x_b[l], 'lam': lru_lambda[l],
            'w_lru_out': w_lru_out[l], 'w_out': w_out[l], 'router_w': router_w[l],
            'w1': expert_w1[l], 'w3': expert_w3[l], 'w2': expert_w2[l],
        }
        x, xc = hybrid_layer(x, xc, c, c_ctx, p, l < DEPTH - 1)
    return x
```

```python
import functools

import jax
import jax.numpy as jnp
from jax import lax
from jax.experimental import pallas as pl
from jax.experimental.pallas import tpu as pltpu

F32 = jnp.float32
BF16 = jnp.bfloat16
I32 = jnp.int32

EPS = 1e-6
GRID_W = 64
RET_HEADS = 8
RET_HD = 256
ROPE_BASE = 10000.0
LRU_HEADS = 8
LRU_C = 8.0
CONV_W = 4
CONV_LEFT = 2
N_EXPERTS = 16
EC_CAPACITY = 2

LANES = 128
SUBLANES = 8
VMEM_LIMIT_BYTES = 56 * 1024 * 1024


def _params(*sem):
    return pltpu.CompilerParams(dimension_semantics=sem, vmem_limit_bytes=VMEM_LIMIT_BYTES)


def _rms(x):
    return x * lax.rsqrt(jnp.mean(x * x, axis=-1, keepdims=True) + EPS)


def _sigmoid(x):
    return 1.0 / (1.0 + jnp.exp(-x))


def _gelu_tanh(x):
    return 0.5 * x * (1.0 + jnp.tanh(0.7978845608028654 * (x + 0.044715 * (x * x * x))))


def _ada_kernel(c_ref, w_ref, b_ref, o_ref):
    c = c_ref[...]
    s = (c * _sigmoid(c)).astype(BF16)
    o_ref[...] = jnp.dot(s, w_ref[...].astype(BF16), preferred_element_type=F32) + b_ref[...]


def _ada_mod(cvecs, ada_w, ada_b, tn=512):
    d, w = ada_w.shape
    tn = min(tn, w)
    return pl.pallas_call(
        _ada_kernel,
        grid=(w // tn,),
        in_specs=[
            pl.BlockSpec((SUBLANES, d), lambda j: (0, 0)),
            pl.BlockSpec((d, tn), lambda j: (0, j)),
            pl.BlockSpec((1, tn), lambda j: (0, j)),
        ],
        out_specs=pl.BlockSpec((SUBLANES, tn), lambda j: (0, j)),
        out_shape=jax.ShapeDtypeStruct((SUBLANES, w), F32),
        compiler_params=_params("parallel"),
        name="ada_mod",
    )(cvecs, ada_w, ada_b.reshape(1, w))


def _prenorm_kernel(x_ref, g_ref, sc_ref, sh_ref, o_ref):
    y = _rms(x_ref[0]) * g_ref[...]
    o_ref[0] = (y * (1.0 + sc_ref[0]) + sh_ref[0]).astype(o_ref.dtype)


def _prenorm(x, g, scale, shift, tr=256):
    b, n, d = x.shape
    tr = min(tr, n)
    return pl.pallas_call(
        _prenorm_kernel,
        grid=(b, n // tr),
        in_specs=[
            pl.BlockSpec((1, tr, d), lambda i, j: (i, j, 0)),
            pl.BlockSpec((1, d), lambda i, j: (0, 0)),
            pl.BlockSpec((1, 1, d), lambda i, j: (i, 0, 0)),
            pl.BlockSpec((1, 1, d), lambda i, j: (i, 0, 0)),
        ],
        out_specs=pl.BlockSpec((1, tr, d), lambda i, j: (i, j, 0)),
        out_shape=jax.ShapeDtypeStruct((b, n, d), BF16),
        compiler_params=_params("parallel", "parallel"),
        name="prenorm",
    )(x, g.reshape(1, d), scale.reshape(b, 1, d), shift.reshape(b, 1, d))


def _proj_kernel(*refs, epilogue, scale):
    if epilogue == "rope":
        h_ref, w_ref, cos_ref, sin_ref, o_ref = refs
    else:
        h_ref, w_ref, o_ref = refs
    acc = jnp.dot(h_ref[...], w_ref[...], preferred_element_type=F32)
    if epilogue == "rope":
        for s in range(acc.shape[1] // LANES):
            a = acc[:, s * LANES:(s + 1) * LANES]
            ts = (s * LANES) % RET_HD
            cos = cos_ref[:, ts:ts + LANES]
            sin = sin_ref[:, ts:ts + LANES]
            o_ref[:, s * LANES:(s + 1) * LANES] = (
                a * cos + pltpu.roll(a, LANES // 2, 1) * sin).astype(o_ref.dtype)
        return
    if epilogue == "silu":
        acc = acc * _sigmoid(acc)
    elif epilogue == "gelu":
        acc = _gelu_tanh(acc)
    elif epilogue == "sigmoid":
        acc = _sigmoid(acc)
    elif epilogue == "scale":
        acc = acc * scale
    o_ref[...] = acc.astype(o_ref.dtype)


def _proj(h, w, col_off, width, epilogue="none", out_dtype=BF16, tables=None, rows_per_seq=None,
          scale=1.0, tm=1024, tn=512):
    m, k = h.shape
    tm = min(tm, m)
    tn = min(tn, width)
    if rows_per_seq is not None:
        tm = min(tm, rows_per_seq)
    jo = col_off // tn
    in_specs = [
        pl.BlockSpec((tm, k), lambda i, j: (i, 0)),
        pl.BlockSpec((k, tn), lambda i, j: (0, jo + j)),
    ]
    args = [h, w]
    if epilogue == "rope":
        nblk = rows_per_seq // tm
        tspec = pl.BlockSpec((tm, RET_HD), lambda i, j: (i % nblk, 0))
        in_specs += [tspec, tspec]
        args += list(tables)
    return pl.pallas_call(
        functools.partial(_proj_kernel, epilogue=epilogue, scale=scale),
        grid=(m // tm, width // tn),
        in_specs=in_specs,
        out_specs=pl.BlockSpec((tm, tn), lambda i, j: (i, j)),
        out_shape=jax.ShapeDtypeStruct((m, width), out_dtype),
        compiler_params=_params("parallel", "arbitrary"),
        name="proj_" + epilogue,
    )(*args)


def _rope_tables(n, k_scale):
    quarter = RET_HD // 4
    freqs = ROPE_BASE ** (-jnp.arange(quarter, dtype=F32) / quarter)
    rows = n // GRID_W
    row = jnp.repeat(jnp.arange(rows, dtype=F32), GRID_W)
    col = jnp.tile(jnp.arange(GRID_W, dtype=F32), rows)

    def part(pos):
        ang = pos[:, None] * freqs[None, :]
        c, s = jnp.cos(ang), jnp.sin(ang)
        return jnp.concatenate([c, c], axis=-1), jnp.concatenate([-s, s], axis=-1)

    cr, sr = part(row)
    cc, sc = part(col)
    cos = jnp.concatenate([cr, cc], axis=-1)
    sin = jnp.concatenate([sr, sc], axis=-1)
    return (cos, sin), (cos * k_scale, sin * k_scale)


def _dot_nt(a, b):
    return lax.dot_general(a, b, (((1,), (1,)), ((), ())), preferred_element_type=F32)


def _dot_tn(a, b):
    return lax.dot_general(a, b, (((0,), (0,)), ((), ())), preferred_element_type=F32)


def _ret_kernel(lg_ref, qf_ref, kf_ref, vf_ref, qb_ref, kb_ref, vb_ref, kc_ref, vc_ref,
                of_ref, ob_ref, sf_ref, sb_ref):
    h = pl.program_id(1)
    c = pl.program_id(2)
    lgf = lg_ref[0, h]
    lgb = lg_ref[1, h]
    csz = qf_ref.shape[1]

    @pl.when(c == 0)
    def _():
        kc = kc_ref[0].astype(F32)
        vc = vc_ref[0]
        lc = kc.shape[0]
        m = lax.broadcasted_iota(I32, (lc, 1), 0).astype(F32)
        sf_ref[...] = _dot_tn((kc * jnp.exp((lc - 1.0 - m) * lgf)).astype(BF16), vc)
        sb_ref[...] = _dot_tn((kc * jnp.exp(m * lgb)).astype(BF16), vc)

    qi = lax.broadcasted_iota(I32, (csz, csz), 0)
    ki = lax.broadcasted_iota(I32, (csz, csz), 1)
    j = lax.broadcasted_iota(I32, (csz, 1), 0).astype(F32)

    def direction(q_ref, k_ref, v_ref, o_ref, s_ref, lg, dist, q_pow, k_pow):
        q = q_ref[0]
        k = k_ref[0]
        v = v_ref[0]
        intra = jnp.where(dist >= 0, jnp.exp(jnp.maximum(dist, 0).astype(F32) * lg), 0.0)
        s = _dot_nt(q, k) * intra
        state = s_ref[...]
        o = jnp.dot(s.astype(BF16), v, preferred_element_type=F32)
        o = o + jnp.dot(q, state.astype(BF16), preferred_element_type=F32) * jnp.exp(q_pow * lg)
        o_ref[0] = o
        kd = (k.astype(F32) * jnp.exp(k_pow * lg)).astype(BF16)
        chunk_dec = jnp.exp(jnp.full((1, 1), csz, F32) * lg)
        s_ref[...] = state * chunk_dec + _dot_tn(kd, v)

    direction(qf_ref, kf_ref, vf_ref, of_ref, sf_ref, lgf, qi - ki, j + 1.0, csz - 1.0 - j)
    direction(qb_ref, kb_ref, vb_ref, ob_ref, sb_ref, lgb, ki - qi, csz - j, j)


def _retention(q, k, v, kc, vc, log_gamma, chunk=256):
    b, n, hw = q.shape
    dh = RET_HD
    nh = hw // dh
    lc = kc.shape[1]
    chunk = min(chunk, n)
    nc = n // chunk
    fwd = pl.BlockSpec((1, chunk, dh), lambda bi, hi, ci: (bi, ci, hi))
    bwd = pl.BlockSpec((1, chunk, dh), lambda bi, hi, ci: (bi, nc - 1 - ci, hi))
    ctx = pl.BlockSpec((1, lc, dh), lambda bi, hi, ci: (bi, 0, hi))
    return pl.pallas_call(
        _ret_kernel,
        grid=(b, nh, nc),
        in_specs=[pl.BlockSpec(memory_space=pltpu.SMEM), fwd, fwd, fwd, bwd, bwd, bwd, ctx, ctx],
        out_specs=[fwd, bwd],
        out_shape=[jax.ShapeDtypeStruct((b, n, hw), F32)] * 2,
        scratch_shapes=[pltpu.VMEM((dh, dh), F32), pltpu.VMEM((dh, dh), F32)],
        compiler_params=_params("parallel", "parallel", "arbitrary"),
        name="retention",
    )(log_gamma.astype(F32), q, k, v, q, k, v, kc, vc)


HALO = 16


def _lru_kernel(mf_ref, pf_ref, nf_ref, mb_ref, pb_ref, nb_ref, cw_ref, cb_ref, wg_ref, bg_ref, lam_ref,
                h0f_ref, h0b_ref, hf_ref, hb_ref, a_s, b_s, cf_s, cb_s):
    t = pl.program_id(2)
    nt = pl.num_programs(2)
    tt = mf_ref.shape[1]
    w = mf_ref.shape[2]

    @pl.when(t == 0)
    def _():
        cf_s[...] = h0f_ref[0]
        cb_s[...] = h0b_ref[0]

    def coeffs(m_ref, p_ref, n_ref, has_prev, has_next, d):
        main = m_ref[0].astype(F32)
        prev = jnp.where(has_prev, p_ref[0].astype(F32), 0.0)
        nxt = jnp.where(has_next, n_ref[0].astype(F32), 0.0)
        ext = jnp.concatenate([prev, main, nxt], axis=0)
        u = cb_ref[...]
        for i in range(CONV_W):
            off = HALO + i - CONV_LEFT
            u = u + ext[off:off + tt] * cw_ref[i:i + 1, :]
        g = jnp.dot(u.astype(BF16), wg_ref[d, 0], preferred_element_type=F32) + bg_ref[d:d + 1, :]
        r = _sigmoid(g[:, :w])
        ig = _sigmoid(g[:, w:])
        lam = lam_ref[d:d + 1, :]
        softplus = jnp.maximum(-lam, 0.0) + jnp.log(1.0 + jnp.exp(-jnp.abs(lam)))
        log_a = -LRU_C * r * softplus
        a = jnp.exp(log_a)
        a_s[...] = a
        b_s[...] = jnp.sqrt(-jnp.tanh(log_a) * (a * a + 1.0)) * (ig * u)

    row = lax.broadcasted_iota(I32, (SUBLANES, w), 0)
    ntile = tt // SUBLANES

    coeffs(mf_ref, pf_ref, nf_ref, t > 0, t < nt - 1, 0)

    def fwd_body(i, carry):
        r0 = pl.multiple_of(i * SUBLANES, SUBLANES)
        a = a_s[pl.ds(r0, SUBLANES), :]
        b = b_s[pl.ds(r0, SUBLANES), :]
        for s in (1, 2, 4):
            keep = row >= s
            b = jnp.where(keep, a * pltpu.roll(b, s, 0) + b, b)
            a = jnp.where(keep, a * pltpu.roll(a, s, 0), a)
        hcur = a * carry + b
        hf_ref[0, pl.ds(r0, SUBLANES), :] = hcur
        return hcur[SUBLANES - 1:SUBLANES, :]

    cf_s[...] = lax.fori_loop(0, ntile, fwd_body, cf_s[...])

    coeffs(mb_ref, pb_ref, nb_ref, t < nt - 1, t > 0, 1)

    def bwd_body(i, carry):
        r0 = pl.multiple_of((ntile - 1 - i) * SUBLANES, SUBLANES)
        a = a_s[pl.ds(r0, SUBLANES), :]
        b = b_s[pl.ds(r0, SUBLANES), :]
        for s in (1, 2, 4):
            keep = row < SUBLANES - s
            b = jnp.where(keep, a * pltpu.roll(b, SUBLANES - s, 0) + b, b)
            a = jnp.where(keep, a * pltpu.roll(a, SUBLANES - s, 0), a)
        hcur = a * carry + b
        hb_ref[0, pl.ds(r0, SUBLANES), :] = hcur
        return hcur[0:1, :]

    cb_s[...] = lax.fori_loop(0, ntile, bwd_body, cb_s[...])


def _lru(lx, conv_w, conv_b, wg, bg, lam, h0f, h0b, tt=512):
    b, n, wtot = lx.shape
    nh = wg.shape[1]
    hd = wtot // nh
    tt = min(tt, n)
    nt = n // tt
    hb_per = tt // HALO
    nhalo = n // HALO

    def main_spec(rev):
        return pl.BlockSpec((1, tt, hd), lambda bi, hi, ti: (bi, (nt - 1 - ti) if rev else ti, hi))

    def prev_spec(rev):
        return pl.BlockSpec(
            (1, HALO, hd),
            lambda bi, hi, ti: (bi, jnp.maximum(((nt - 1 - ti) if rev else ti) * hb_per - 1, 0), hi))

    def next_spec(rev):
        return pl.BlockSpec(
            (1, HALO, hd),
            lambda bi, hi, ti: (bi, jnp.minimum((((nt - 1 - ti) if rev else ti) + 1) * hb_per, nhalo - 1), hi))

    head_vec = lambda rows: pl.BlockSpec((rows, hd), lambda bi, hi, ti: (0, hi))
    state = pl.BlockSpec((1, 1, hd), lambda bi, hi, ti: (bi, 0, hi))
    bg2 = bg.reshape(2, nh * 2 * hd)
    return pl.pallas_call(
        _lru_kernel,
        grid=(b, nh, nt),
        in_specs=[
            main_spec(False), prev_spec(False), next_spec(False),
            main_spec(True), prev_spec(True), next_spec(True),
            head_vec(CONV_W), head_vec(1),
            pl.BlockSpec((2, 1, hd, 2 * hd), lambda bi, hi, ti: (0, hi, 0, 0)),
            pl.BlockSpec((2, 2 * hd), lambda bi, hi, ti: (0, hi)),
            head_vec(2), state, state,
        ],
        out_specs=[main_spec(False), main_spec(True)],
        out_shape=[jax.ShapeDtypeStruct((b, n, wtot), F32)] * 2,
        scratch_shapes=[pltpu.VMEM((tt, hd), F32), pltpu.VMEM((tt, hd), F32),
                        pltpu.VMEM((1, hd), F32), pltpu.VMEM((1, hd), F32)],
        compiler_params=_params("parallel", "parallel", "arbitrary"),
        name="rglru",
    )(lx, lx, lx, lx, lx, lx, conv_w, conv_b.reshape(1, wtot), wg, bg2, lam,
      h0f.reshape(b, 1, wtot), h0b.reshape(b, 1, wtot))


def _mergeprep_kernel(of_ref, ob_ref, g_ref, hf_ref, hb_ref, ly_ref, o_ref):
    rw = of_ref.shape[1]
    for h in range(rw // RET_HD):
        sl = slice(h * RET_HD, (h + 1) * RET_HD)
        o = _rms(of_ref[:, sl] + ob_ref[:, sl])
        o_ref[:, sl] = (o * g_ref[:, sl].astype(F32)).astype(o_ref.dtype)
    o_ref[:, rw:] = (ly_ref[...].astype(F32) * (hf_ref[...] + hb_ref[...])).astype(o_ref.dtype)


def _mergeprep(of, ob, g, hf, hb, ly, tr=256):
    m, rw = of.shape
    lw = hf.shape[1]
    tr = min(tr, m)
    spec = lambda wd: pl.BlockSpec((tr, wd), lambda i: (i, 0))
    return pl.pallas_call(
        _mergeprep_kernel,
        grid=(m // tr,),
        in_specs=[spec(rw), spec(rw), spec(rw), spec(lw), spec(lw), spec(lw)],
        out_specs=spec(rw + lw),
        out_shape=jax.ShapeDtypeStruct((m, rw + lw), BF16),
        compiler_params=_params("parallel"),
        name="mergeprep",
    )(of, ob, g, hf, hb, ly)


def _merge_kernel(a_ref, wr_ref, wl_ref, gr_ref, gl_ref, o_ref):
    rw = wr_ref.shape[0]
    ret = jnp.dot(a_ref[:, :rw], wr_ref[...], preferred_element_type=F32)
    lru = jnp.dot(a_ref[:, rw:], wl_ref[...], preferred_element_type=F32)
    o_ref[...] = (gr_ref[...].astype(F32) * ret + gl_ref[...].astype(F32) * lru).astype(o_ref.dtype)


def _merge(a, w_ret_out, w_lru_out, gates, tm=1024, tn=512):
    m = a.shape[0]
    rw, d = w_ret_out.shape
    lw = w_lru_out.shape[0]
    tm = min(tm, m)
    tn = min(tn, d)
    nj = d // tn
    return pl.pallas_call(
        _merge_kernel,
        grid=(m // tm, nj),
        in_specs=[
            pl.BlockSpec((tm, rw + lw), lambda i, j: (i, 0)),
            pl.BlockSpec((rw, tn), lambda i, j: (0, j)),
            pl.BlockSpec((lw, tn), lambda i, j: (0, j)),
            pl.BlockSpec((tm, tn), lambda i, j: (i, j)),
            pl.BlockSpec((tm, tn), lambda i, j: (i, nj + j)),
        ],
        out_specs=pl.BlockSpec((tm, tn), lambda i, j: (i, j)),
        out_shape=jax.ShapeDtypeStruct((m, d), BF16),
        compiler_params=_params("parallel", "arbitrary"),
        name="merge",
    )(a, w_ret_out, w_lru_out, gates, gates)


def _post_mix_kernel(x_ref, mix_ref, gt_ref, g1_ref, g2_ref, sc_ref, sh_ref, x1_ref, h2_ref):
    x1 = x_ref[0] + gt_ref[0] * (_rms(mix_ref[0]) * g1_ref[...])
    x1_ref[0] = x1
    h2_ref[0] = (_rms(x1) * g2_ref[...]) * (1.0 + sc_ref[0]) + sh_ref[0]


def _post_mix(x, mix, gate, g_post, g_pre, scale, shift, tr=256):
    b, n, d = x.shape
    tr = min(tr, n)
    rows = pl.BlockSpec((1, tr, d), lambda i, j: (i, j, 0))
    vec = pl.BlockSpec((1, d), lambda i, j: (0, 0))
    bvec = pl.BlockSpec((1, 1, d), lambda i, j: (i, 0, 0))
    return pl.pallas_call(
        _post_mix_kernel,
        grid=(b, n // tr),
        in_specs=[rows, rows, bvec, vec, vec, bvec, bvec],
        out_specs=[rows, rows],
        out_shape=[jax.ShapeDtypeStruct((b, n, d), F32)] * 2,
        compiler_params=_params("parallel", "parallel"),
        name="post_mix",
    )(x, mix, gate.reshape(b, 1, d), g_post.reshape(1, d), g_pre.reshape(1, d),
      scale.reshape(b, 1, d), shift.reshape(b, 1, d))


def _logits_kernel(rw_ref, h_ref, o_ref):
    o_ref[0] = _dot_nt(rw_ref[...], h_ref[0].astype(BF16))


def _router_logits(h2, router_w_t, tn=512):
    b, n, d = h2.shape
    e = router_w_t.shape[0]
    tn = min(tn, n)
    return pl.pallas_call(
        _logits_kernel,
        grid=(b, n // tn),
        in_specs=[pl.BlockSpec((e, d), lambda i, j: (0, 0)),
                  pl.BlockSpec((1, tn, d), lambda i, j: (i, j, 0))],
        out_specs=pl.BlockSpec((1, e, tn), lambda i, j: (i, 0, j)),
        out_shape=jax.ShapeDtypeStruct((b, e, n), F32),
        compiler_params=_params("parallel", "parallel"),
        name="router_logits",
    )(router_w_t, h2)


def _lane_cumsum_excl(m):
    e, n = m.shape
    tri = (lax.broadcasted_iota(I32, (LANES, LANES), 0) < lax.broadcasted_iota(I32, (LANES, LANES), 1)).astype(BF16)
    carry = jnp.zeros((e, 1), F32)
    outs = []
    for c in range(n // LANES):
        blk = m[:, c * LANES:(c + 1) * LANES]
        outs.append(jnp.dot(blk.astype(BF16), tri, preferred_element_type=F32) + carry)
        carry = carry + jnp.sum(blk, axis=1, keepdims=True)
    return jnp.concatenate(outs, axis=1)


SEL_SLOTS = 256
SEL_TOKENS = 1024


def _sel_kernel(lg_ref, idx_ref, gate_ref, aff_s, cum_s, m_s, *, cap):
    lg = lg_ref[0]
    ne, n = lg.shape
    ex = jnp.exp(lg - jnp.max(lg, axis=0, keepdims=True))
    aff = ex / jnp.sum(ex, axis=0, keepdims=True)
    bits = lax.bitcast_convert_type(aff, I32)

    thr = jnp.zeros((ne, 1), I32)
    for bit in range(30, -1, -1):
        cand = thr | (1 << bit)
        cnt = jnp.sum((bits >= cand).astype(I32), axis=1, keepdims=True)
        thr = jnp.where(cnt >= cap, cand, thr)

    above = bits > thr
    tied = (bits == thr).astype(F32)
    need = (cap - jnp.sum(above.astype(I32), axis=1, keepdims=True)).astype(F32)
    sel = jnp.where(above, 1.0, jnp.where(_lane_cumsum_excl(tied) < need, tied, 0.0))
    aff_s[...] = aff
    m_s[...] = sel
    cum_s[...] = _lane_cumsum_excl(sel) + sel

    ss = min(SEL_SLOTS, cap)
    st = min(SEL_TOKENS, n)
    lane = lax.broadcasted_iota(I32, (ss, LANES), 1)
    s_base = lax.broadcasted_iota(I32, (ss, st), 0).astype(F32)
    idx_ref[...] = jnp.zeros(idx_ref.shape, idx_ref.dtype)
    gate_ref[...] = jnp.zeros(gate_ref.shape, gate_ref.dtype)

    def fold(v):
        out = v[:, :LANES]
        for q in range(1, st // LANES):
            out = out + v[:, q * LANES:(q + 1) * LANES]
        return out

    def per_expert(e, _):
        for sb in range(cap // ss):
            sv = s_base + float(sb * ss)

            def per_chunk(c, carry):
                ia, ga = carry
                t0 = pl.multiple_of(c * st, st)
                ce = cum_s[pl.ds(e, 1), pl.ds(t0, st)]
                me = m_s[pl.ds(e, 1), pl.ds(t0, st)]
                ae = aff_s[pl.ds(e, 1), pl.ds(t0, st)]
                ia = ia + fold((ce <= sv).astype(I32))
                hit = jnp.logical_and(ce == sv + 1.0, me > 0.5)
                ga = ga + fold(jnp.where(hit, ae, 0.0))
                return ia, ga

            ia, ga = lax.fori_loop(
                0, n // st, per_chunk,
                (jnp.zeros((ss, LANES), I32), jnp.zeros((ss, LANES), F32)))
            rows = slice(sb * ss, (sb + 1) * ss)
            mine = lane == e
            idx_ref[0, rows, :] = jnp.where(mine, jnp.sum(ia, axis=1, keepdims=True), idx_ref[0, rows, :])
            gate_ref[0, rows, :] = jnp.where(mine, jnp.sum(ga, axis=1, keepdims=True), gate_ref[0, rows, :])
        return 0

    lax.fori_loop(0, ne, per_expert, 0)


def _expert_choice(logits, cap):
    b, ne, n = logits.shape
    return pl.pallas_call(
        functools.partial(_sel_kernel, cap=cap),
        grid=(b,),
        in_specs=[pl.BlockSpec((1, ne, n), lambda i: (i, 0, 0))],
        out_specs=[pl.BlockSpec((1, cap, LANES), lambda i: (i, 0, 0))] * 2,
        out_shape=[jax.ShapeDtypeStruct((b, cap, LANES), I32), jax.ShapeDtypeStruct((b, cap, LANES), F32)],
        scratch_shapes=[pltpu.VMEM((ne, n), F32)] * 3,
        compiler_params=_params("parallel"),
        name="expert_choice",
    )(logits)


def _row_copy(src_hbm, dst, src_row, dst_row, sem):
    return pltpu.make_async_copy(src_hbm.at[pl.ds(src_row, 1), :], dst.at[pl.ds(dst_row, 1), :], sem)


def _gather_kernel(rows_ref, h_hbm, o_ref, buf, sem):
    tg = buf.shape[0]
    base = pl.program_id(0) * tg

    def issue(r, _):
        _row_copy(h_hbm, buf, rows_ref[base + r], r, sem).start()
        return 0

    def wait(r, _):
        _row_copy(h_hbm, buf, 0, r, sem).wait()
        return 0

    lax.fori_loop(0, tg, issue, 0)
    lax.fori_loop(0, tg, wait, 0)
    o_ref[...] = buf[...].astype(o_ref.dtype)


def _gather_rows(h2, rows, tg=256):
    d = h2.shape[1]
    s = rows.shape[0]
    tg = min(tg, s)
    return pl.pallas_call(
        _gather_kernel,
        grid_spec=pltpu.PrefetchScalarGridSpec(
            num_scalar_prefetch=1,
            grid=(s // tg,),
            in_specs=[pl.BlockSpec(memory_space=pl.ANY)],
            out_specs=pl.BlockSpec((tg, d), lambda i, rows_ref: (i, 0)),
            scratch_shapes=[pltpu.VMEM((tg, d), F32), pltpu.SemaphoreType.DMA],
        ),
        out_shape=jax.ShapeDtypeStruct((s, d), BF16),
        compiler_params=_params("arbitrary"),
        name="gather_rows",
    )(rows, h2)


def _ffn_kernel(rows_ref, xg_ref, w1_ref, w3_ref, w2_ref, gate_ref, zero_hbm, moe_hbm, acc, rowbuf, sem):
    del zero_hbm
    tm = acc.shape[0]
    f = pl.program_id(1)
    base = pl.program_id(0) * tm

    @pl.when(f == 0)
    def _():
        acc[...] = jnp.zeros(acc.shape, acc.dtype)

    x = xg_ref[...]
    h1 = jnp.dot(x, w1_ref[0], preferred_element_type=F32)
    h3 = jnp.dot(x, w3_ref[0], preferred_element_type=F32)
    hid = (h1 * _sigmoid(h1) * h3).astype(BF16)
    acc[...] += jnp.dot(hid, w2_ref[0], preferred_element_type=F32)

    @pl.when(f == pl.num_programs(1) - 1)
    def _():
        def fetch(r, _):
            _row_copy(moe_hbm, rowbuf, rows_ref[base + r], r, sem).start()
            return 0

        def fetched(r, _):
            _row_copy(moe_hbm, rowbuf, 0, r, sem).wait()
            return 0

        lax.fori_loop(0, tm, fetch, 0)
        lax.fori_loop(0, tm, fetched, 0)
        rowbuf[...] += acc[...] * gate_ref[...]

        def put(r, _):
            pltpu.make_async_copy(rowbuf.at[pl.ds(r, 1), :], moe_hbm.at[pl.ds(rows_ref[base + r], 1), :], sem).start()
            return 0

        def done(r, _):
            pltpu.make_async_copy(rowbuf.at[pl.ds(r, 1), :], moe_hbm.at[pl.ds(0, 1), :], sem).wait()
            return 0

        lax.fori_loop(0, tm, put, 0)
        lax.fori_loop(0, tm, done, 0)


def _expert_ffn(xg, w1, w3, w2, rows, gate, out_rows, tm=512, tf=256):
    ne, d, ff = w1.shape
    s = xg.shape[0] // ne
    tm = min(tm, s)
    tf = min(tf, ff)
    per_e = s // tm
    zeros = jnp.zeros((out_rows, d), F32)
    return pl.pallas_call(
        _ffn_kernel,
        grid_spec=pltpu.PrefetchScalarGridSpec(
            num_scalar_prefetch=1,
            grid=(ne * per_e, ff // tf),
            in_specs=[
                pl.BlockSpec((tm, d), lambda i, f, r: (i, 0)),
                pl.BlockSpec((1, d, tf), lambda i, f, r: (i // per_e, 0, f)),
                pl.BlockSpec((1, d, tf), lambda i, f, r: (i // per_e, 0, f)),
                pl.BlockSpec((1, tf, d), lambda i, f, r: (i // per_e, f, 0)),
                pl.BlockSpec((tm, 1), lambda i, f, r: (i, 0)),
                pl.BlockSpec(memory_space=pl.ANY),
            ],
            out_specs=pl.BlockSpec(memory_space=pl.ANY),
            scratch_shapes=[pltpu.VMEM((tm, d), F32), pltpu.VMEM((tm, d), F32), pltpu.SemaphoreType.DMA],
        ),
        out_shape=jax.ShapeDtypeStruct((out_rows, d), F32),
        input_output_aliases={6: 0},
        compiler_params=_params("arbitrary", "arbitrary"),
        name="expert_ffn",
    )(rows, xg, w1, w3, w2, gate, zeros)


def _final_kernel(x_ref, moe_ref, gt_ref, g_ref, o_ref):
    o_ref[0] = x_ref[0] + gt_ref[0] * (_rms(moe_ref[0]) * g_ref[...])


def _final(x1, moe, gate, g, tr=256):
    b, n, d = x1.shape
    tr = min(tr, n)
    rows = pl.BlockSpec((1, tr, d), lambda i, j: (i, j, 0))
    return pl.pallas_call(
        _final_kernel,
        grid=(b, n // tr),
        in_specs=[rows, rows, pl.BlockSpec((1, 1, d), lambda i, j: (i, 0, 0)),
                  pl.BlockSpec((1, d), lambda i, j: (0, 0))],
        out_specs=rows,
        out_shape=jax.ShapeDtypeStruct((b, n, d), F32),
        compiler_params=_params("parallel", "parallel"),
        name="final_residual",
    )(x1, moe, gate.reshape(b, 1, d), g.reshape(1, d))


def _layer(x, c, ctx, c_ctx, p):
    b, n, d = x.shape
    lc = ctx.shape[1]
    rw = RET_HEADS * RET_HD
    lw = p["conv_w"].shape[1]
    q_off, k_off, v_off, g_off = 0, rw, 2 * rw, 3 * rw
    lx_off = 4 * rw
    ly_off = lx_off + lw
    gr_off = ly_off + lw

    cvecs = jnp.zeros((SUBLANES, d), F32).at[:b].set(c).at[b].set(c_ctx)
    mod = _ada_mod(cvecs, p["ada_w"], p["ada_b"])
    sh1, sc1, gt1, sh2, sc2, gt2 = [mod[:b, i * d:(i + 1) * d] for i in range(6)]
    csh1 = jnp.broadcast_to(mod[b:b + 1, :d], (b, d))
    csc1 = jnp.broadcast_to(mod[b:b + 1, d:2 * d], (b, d))

    w_in = p["w_in"].astype(BF16)
    h = _prenorm(x, p["n_pre_mix"], sc1, sh1).reshape(b * n, d)
    hc = _prenorm(ctx, p["n_pre_mix"], csc1, csh1).reshape(b * lc, d)

    k_scale = RET_HD ** -0.5
    q_tabs, k_tabs = _rope_tables(n, k_scale)
    q = _proj(h, w_in, q_off, rw, "rope", tables=q_tabs, rows_per_seq=n)
    k = _proj(h, w_in, k_off, rw, "rope", tables=k_tabs, rows_per_seq=n)
    v = _proj(h, w_in, v_off, rw)
    g = _proj(h, w_in, g_off, rw, "silu")
    lx = _proj(h, w_in, lx_off, lw)
    ly = _proj(h, w_in, ly_off, lw, "gelu")
    gates = _proj(h, w_in, gr_off, 2 * d, "sigmoid")
    kc = _proj(hc, w_in, k_off, rw, "scale", scale=k_scale)
    vc = _proj(hc, w_in, v_off, rw)
    lxc = _proj(hc, w_in, lx_off, lw)

    of, ob = _retention(q.reshape(b, n, rw), k.reshape(b, n, rw), v.reshape(b, n, rw),
                        kc.reshape(b, lc, rw), vc.reshape(b, lc, rw), p["ret_lg"])

    wg = jnp.concatenate([p["gate_a_w"], p["gate_x_w"]], axis=-1).astype(BF16)
    hd = lw // LRU_HEADS
    bg = jnp.concatenate([p["gate_a_b"].reshape(2, LRU_HEADS, hd), p["gate_x_b"].reshape(2, LRU_HEADS, hd)], axis=-1)
    lru_w = (p["conv_w"], p["conv_b"], wg, bg, p["lam"])
    zero_state = jnp.zeros((b, lw), F32)
    hfc, hbc = _lru(lxc.reshape(b, lc, lw), *lru_w, zero_state, zero_state)
    hf, hb = _lru(lx.reshape(b, n, lw), *lru_w, hfc[:, -1], hbc[:, 0])

    a = _mergeprep(of.reshape(b * n, rw), ob.reshape(b * n, rw), g,
                   hf.reshape(b * n, lw), hb.reshape(b * n, lw), ly)
    merged = _merge(a, p["w_ret_out"].astype(BF16), p["w_lru_out"].astype(BF16), gates)
    mix = _proj(merged, p["w_out"].astype(BF16), 0, d, out_dtype=F32)
    x1, h2 = _post_mix(x, mix.reshape(b, n, d), gt1, p["n_post_mix"], p["n_pre_ffn"], sc2, sh2)

    ne = p["router_w"].shape[1]
    cap = EC_CAPACITY * n // ne
    logits = _router_logits(h2, p["router_w"].T.astype(BF16))
    idx, gate = _expert_choice(logits, cap)
    idx = jnp.transpose(idx[:, :, :ne], (2, 0, 1))
    gate = jnp.transpose(gate[:, :, :ne], (2, 0, 1))
    rows = (idx + (jnp.arange(b, dtype=I32) * n)[None, :, None]).reshape(-1)
    xg = _gather_rows(h2.reshape(b * n, d), rows)
    moe = _expert_ffn(xg, p["w1"].astype(BF16), p["w3"].astype(BF16), p["w2"].astype(BF16),
                      rows, gate.reshape(-1, 1), b * n)
    return _final(x1, moe.reshape(b, n, d), gt2, p["n_post_ffn"])


def kernel(x, c, ctx, c_ctx, ada_w, ada_b, norm_pre_mix, norm_post_mix, norm_pre_ffn, norm_post_ffn, w_in, ret_log_gamma, w_ret_out, lru_conv_w, lru_conv_b, lru_gate_a_w, lru_gate_a_b, lru_gate_x_w, lru_gate_x_b, lru_lambda, w_lru_out, w_out, router_w, expert_w1, expert_w3, expert_w2):
    depth = ada_w.shape[0]
    assert depth == 1, "context-stream update between layers is not implemented"
    l = 0
    p = {
        "ada_w": ada_w[l], "ada_b": ada_b[l],
        "n_pre_mix": norm_pre_mix[l], "n_post_mix": norm_post_mix[l],
        "n_pre_ffn": norm_pre_ffn[l], "n_post_ffn": norm_post_ffn[l],
        "w_in": w_in[l], "ret_lg": ret_log_gamma[l], "w_ret_out": w_ret_out[l],
        "conv_w": lru_conv_w[l], "conv_b": lru_conv_b[l],
        "gate_a_w": lru_gate_a_w[l], "gate_a_b": lru_gate_a_b[l],
        "gate_x_w": lru_gate_x_w[l], "gate_x_b": lru_gate_x_b[l], "lam": lru_lambda[l],
        "w_lru_out": w_lru_out[l], "w_out": w_out[l], "router_w": router_w[l],
        "w1": expert_w1[l], "w3": expert_w3[l], "w2": expert_w2[l],
    }
    return _layer(x, c, ctx, c_ctx, p)
```

```python
import functools

import jax
import jax.numpy as jnp
from jax import lax
from jax.experimental import pallas as pl
from jax.experimental.pallas import tpu as pltpu

F32 = jnp.float32
BF16 = jnp.bfloat16
I32 = jnp.int32

EPS = 1e-6
GRID_W = 64
RET_HEADS = 8
RET_HD = 256
ROPE_BASE = 10000.0
LRU_HEADS = 8
LRU_C = 8.0
CONV_W = 4
CONV_LEFT = 2
N_EXPERTS = 16
EC_CAPACITY = 2

LANES = 128
SUBLANES = 8
VMEM_LIMIT_BYTES = 56 * 1024 * 1024


def _params(*sem):
    return pltpu.CompilerParams(dimension_semantics=sem, vmem_limit_bytes=VMEM_LIMIT_BYTES)


def _rms(x):
    return x * lax.rsqrt(jnp.mean(x * x, axis=-1, keepdims=True) + EPS)


def _sigmoid(x):
    return 0.5 * jnp.tanh(0.5 * x) + 0.5


def _gelu_tanh(x):
    return 0.5 * x * (1.0 + jnp.tanh(0.7978845608028654 * (x + 0.044715 * (x * x * x))))


def _ada_kernel(c_ref, w_ref, b_ref, o_ref):
    c = c_ref[...]
    s = (c * _sigmoid(c)).astype(BF16)
    o_ref[...] = jnp.dot(s, w_ref[...].astype(BF16), preferred_element_type=F32) + b_ref[...]


def _ada_mod(cvecs, ada_w, ada_b, tn=512):
    d, w = ada_w.shape
    tn = min(tn, w)
    return pl.pallas_call(
        _ada_kernel,
        grid=(w // tn,),
        in_specs=[
            pl.BlockSpec((SUBLANES, d), lambda j: (0, 0)),
            pl.BlockSpec((d, tn), lambda j: (0, j)),
            pl.BlockSpec((1, tn), lambda j: (0, j)),
        ],
        out_specs=pl.BlockSpec((SUBLANES, tn), lambda j: (0, j)),
        out_shape=jax.ShapeDtypeStruct((SUBLANES, w), F32),
        compiler_params=_params("parallel"),
        name="ada_mod",
    )(cvecs, ada_w, ada_b.reshape(1, w))


def _prenorm_kernel(x_ref, g_ref, sc_ref, sh_ref, o_ref):
    y = _rms(x_ref[0]) * g_ref[...]
    o_ref[0] = (y * (1.0 + sc_ref[0]) + sh_ref[0]).astype(o_ref.dtype)


def _prenorm(x, g, scale, shift, tr=256):
    b, n, d = x.shape
    tr = min(tr, n)
    return pl.pallas_call(
        _prenorm_kernel,
        grid=(b, n // tr),
        in_specs=[
            pl.BlockSpec((1, tr, d), lambda i, j: (i, j, 0)),
            pl.BlockSpec((1, d), lambda i, j: (0, 0)),
            pl.BlockSpec((1, 1, d), lambda i, j: (i, 0, 0)),
            pl.BlockSpec((1, 1, d), lambda i, j: (i, 0, 0)),
        ],
        out_specs=pl.BlockSpec((1, tr, d), lambda i, j: (i, j, 0)),
        out_shape=jax.ShapeDtypeStruct((b, n, d), BF16),
        compiler_params=_params("parallel", "parallel"),
        name="prenorm",
    )(x, g.reshape(1, d), scale.reshape(b, 1, d), shift.reshape(b, 1, d))


def _proj_kernel(*refs, epilogue, scale):
    if epilogue == "rope":
        h_ref, w_ref, cos_ref, sin_ref, o_ref = refs
    else:
        h_ref, w_ref, o_ref = refs
    acc = jnp.dot(h_ref[...], w_ref[...].astype(BF16), preferred_element_type=F32)
    if epilogue == "rope":
        for s in range(acc.shape[1] // LANES):
            a = acc[:, s * LANES:(s + 1) * LANES]
            ts = (s * LANES) % RET_HD
            cos = cos_ref[:, ts:ts + LANES]
            sin = sin_ref[:, ts:ts + LANES]
            o_ref[:, s * LANES:(s + 1) * LANES] = (
                a * cos + pltpu.roll(a, LANES // 2, 1) * sin).astype(o_ref.dtype)
        return
    if epilogue == "silu":
        acc = acc * _sigmoid(acc)
    elif epilogue == "gelu":
        acc = _gelu_tanh(acc)
    elif epilogue == "sigmoid":
        acc = _sigmoid(acc)
    elif epilogue == "scale":
        acc = acc * scale
    o_ref[...] = acc.astype(o_ref.dtype)


def _proj(h, w, col_off, width, epilogue="none", out_dtype=BF16, tables=None, rows_per_seq=None,
          scale=1.0, tm=1024, tn=512):
    m, k = h.shape
    tm = min(tm, m)
    tn = min(tn, width)
    if rows_per_seq is not None:
        tm = min(tm, rows_per_seq)
    jo = col_off // tn
    in_specs = [
        pl.BlockSpec((tm, k), lambda i, j: (i, 0)),
        pl.BlockSpec((k, tn), lambda i, j: (0, jo + j)),
    ]
    args = [h, w]
    if epilogue == "rope":
        nblk = rows_per_seq // tm
        tspec = pl.BlockSpec((tm, RET_HD), lambda i, j: (i % nblk, 0))
        in_specs += [tspec, tspec]
        args += list(tables)
    return pl.pallas_call(
        functools.partial(_proj_kernel, epilogue=epilogue, scale=scale),
        grid=(m // tm, width // tn),
        in_specs=in_specs,
        out_specs=pl.BlockSpec((tm, tn), lambda i, j: (i, j)),
        out_shape=jax.ShapeDtypeStruct((m, width), out_dtype),
        compiler_params=_params("parallel", "arbitrary"),
        name="proj_" + epilogue,
    )(*args)


def _rope_tables(n, k_scale):
    quarter = RET_HD // 4
    freqs = ROPE_BASE ** (-jnp.arange(quarter, dtype=F32) / quarter)
    rows = n // GRID_W
    row = jnp.repeat(jnp.arange(rows, dtype=F32), GRID_W)
    col = jnp.tile(jnp.arange(GRID_W, dtype=F32), rows)

    def part(pos):
        ang = pos[:, None] * freqs[None, :]
        c, s = jnp.cos(ang), jnp.sin(ang)
        return jnp.concatenate([c, c], axis=-1), jnp.concatenate([-s, s], axis=-1)

    cr, sr = part(row)
    cc, sc = part(col)
    cos = jnp.concatenate([cr, cc], axis=-1)
    sin = jnp.concatenate([sr, sc], axis=-1)
    return (cos, sin), (cos * k_scale, sin * k_scale)


def _dot_nt(a, b):
    return lax.dot_general(a, b, (((1,), (1,)), ((), ())), preferred_element_type=F32)


def _dot_tn(a, b):
    return lax.dot_general(a, b, (((0,), (0,)), ((), ())), preferred_element_type=F32)


def _ret_kernel(lg_ref, qf_ref, kf_ref, vf_ref, qb_ref, kb_ref, vb_ref, kc_ref, vc_ref,
                of_ref, ob_ref, sf_ref, sb_ref, *, csz):
    h = pl.program_id(1)
    c = pl.program_id(2)
    lgf = lg_ref[0, h]
    lgb = lg_ref[1, h]
    nsub = qf_ref.shape[1] // csz

    @pl.when(c == 0)
    def _():
        kc = kc_ref[0].astype(F32)
        vc = vc_ref[0]
        lc = kc.shape[0]
        m = lax.broadcasted_iota(I32, (lc, 1), 0).astype(F32)
        sf_ref[...] = _dot_tn((kc * jnp.exp((lc - 1.0 - m) * lgf)).astype(BF16), vc)
        sb_ref[...] = _dot_tn((kc * jnp.exp(m * lgb)).astype(BF16), vc)

    qi = lax.broadcasted_iota(I32, (csz, csz), 0)
    ki = lax.broadcasted_iota(I32, (csz, csz), 1)
    j = lax.broadcasted_iota(I32, (csz, 1), 0).astype(F32)

    def direction(q_ref, k_ref, v_ref, o_ref, s_ref, lg, dist, q_pow, k_pow, order):
        intra = jnp.where(dist >= 0, jnp.exp(jnp.maximum(dist, 0).astype(F32) * lg), 0.0)
        q_dec = jnp.exp(q_pow * lg)
        k_dec = jnp.exp(k_pow * lg)
        chunk_dec = jnp.exp(jnp.full((1, 1), csz, F32) * lg)
        state = s_ref[...]
        for sub in order:
            rows = slice(sub * csz, (sub + 1) * csz)
            q = q_ref[0, rows, :]
            k = k_ref[0, rows, :]
            v = v_ref[0, rows, :]
            s = _dot_nt(q, k) * intra
            o = jnp.dot(s.astype(BF16), v, preferred_element_type=F32)
            o_ref[0, rows, :] = o + jnp.dot(q, state.astype(BF16), preferred_element_type=F32) * q_dec
            kd = (k.astype(F32) * k_dec).astype(BF16)
            state = state * chunk_dec + _dot_tn(kd, v)
        s_ref[...] = state

    direction(qf_ref, kf_ref, vf_ref, of_ref, sf_ref, lgf, qi - ki, j + 1.0, csz - 1.0 - j, range(nsub))
    direction(qb_ref, kb_ref, vb_ref, ob_ref, sb_ref, lgb, ki - qi, csz - j, j, range(nsub - 1, -1, -1))


def _retention(q, k, v, kc, vc, log_gamma, chunk=256, block=1024):
    b, n, hw = q.shape
    dh = RET_HD
    nh = hw // dh
    lc = kc.shape[1]
    chunk = min(chunk, n)
    block = min(block, n)
    nc = n // block
    fwd = pl.BlockSpec((1, block, dh), lambda bi, hi, ci: (bi, ci, hi))
    bwd = pl.BlockSpec((1, block, dh), lambda bi, hi, ci: (bi, nc - 1 - ci, hi))
    ctx = pl.BlockSpec((1, lc, dh), lambda bi, hi, ci: (bi, 0, hi))
    return pl.pallas_call(
        functools.partial(_ret_kernel, csz=chunk),
        grid=(b, nh, nc),
        in_specs=[pl.BlockSpec(memory_space=pltpu.SMEM), fwd, fwd, fwd, bwd, bwd, bwd, ctx, ctx],
        out_specs=[fwd, bwd],
        out_shape=[jax.ShapeDtypeStruct((b, n, hw), F32)] * 2,
        scratch_shapes=[pltpu.VMEM((dh, dh), F32), pltpu.VMEM((dh, dh), F32)],
        compiler_params=_params("parallel", "parallel", "arbitrary"),
        name="retention",
    )(log_gamma.astype(F32), q, k, v, q, k, v, kc, vc)


HALO = SUBLANES
SCAN_UNROLL = 8


def _lru_kernel(mf_ref, pf_ref, nf_ref, mb_ref, pb_ref, nb_ref, cw_ref, cb_ref, wg_ref, bg_ref, lam_ref,
                h0f_ref, h0b_ref, hf_ref, hb_ref, ext_s, a_s, b_s, cf_s, cb_s):
    t = pl.program_id(2)
    nt = pl.num_programs(2)
    tt = mf_ref.shape[1]
    w = mf_ref.shape[2]

    @pl.when(t == 0)
    def _():
        cf_s[...] = h0f_ref[0]
        cb_s[...] = h0b_ref[0]

    def coeffs(m_ref, p_ref, n_ref, has_prev, has_next, d):
        ext_s[0:HALO, :] = jnp.where(has_prev, p_ref[0], 0.0)
        ext_s[HALO:HALO + tt, :] = m_ref[0]
        ext_s[HALO + tt:, :] = jnp.where(has_next, n_ref[0], 0.0)
        u = cb_ref[...]
        for i in range(CONV_W):
            off = HALO + i - CONV_LEFT
            u = u + ext_s[off:off + tt, :] * cw_ref[i:i + 1, :]
        g = jnp.dot(u.astype(BF16), wg_ref[d, 0], preferred_element_type=F32) + bg_ref[d:d + 1, :]
        r = _sigmoid(g[:, :w])
        ig = _sigmoid(g[:, w:])
        lam = lam_ref[d:d + 1, :]
        softplus = jnp.maximum(-lam, 0.0) + jnp.log(1.0 + jnp.exp(-jnp.abs(lam)))
        log_a = -LRU_C * r * softplus
        a = jnp.exp(log_a)
        a_s[...] = a
        b_s[...] = jnp.sqrt(-jnp.tanh(log_a) * (a * a + 1.0)) * (ig * u)

    row = lax.broadcasted_iota(I32, (SUBLANES, w), 0)
    ntile = tt // SUBLANES

    coeffs(mf_ref, pf_ref, nf_ref, t > 0, t < nt - 1, 0)

    def fwd_body(i, carry):
        r0 = pl.multiple_of(i * SUBLANES, SUBLANES)
        a = a_s[pl.ds(r0, SUBLANES), :]
        b = b_s[pl.ds(r0, SUBLANES), :]
        for s in (1, 2, 4):
            keep = row >= s
            b = jnp.where(keep, a * pltpu.roll(b, s, 0) + b, b)
            a = jnp.where(keep, a * pltpu.roll(a, s, 0), a)
        hcur = a * carry + b
        hf_ref[0, pl.ds(r0, SUBLANES), :] = hcur
        return hcur[SUBLANES - 1:SUBLANES, :]

    cf_s[...] = lax.fori_loop(0, ntile, fwd_body, cf_s[...], unroll=min(SCAN_UNROLL, ntile))

    coeffs(mb_ref, pb_ref, nb_ref, t < nt - 1, t > 0, 1)

    def bwd_body(i, carry):
        r0 = pl.multiple_of((ntile - 1 - i) * SUBLANES, SUBLANES)
        a = a_s[pl.ds(r0, SUBLANES), :]
        b = b_s[pl.ds(r0, SUBLANES), :]
        for s in (1, 2, 4):
            keep = row < SUBLANES - s
            b = jnp.where(keep, a * pltpu.roll(b, SUBLANES - s, 0) + b, b)
            a = jnp.where(keep, a * pltpu.roll(a, SUBLANES - s, 0), a)
        hcur = a * carry + b
        hb_ref[0, pl.ds(r0, SUBLANES), :] = hcur
        return hcur[0:1, :]

    cb_s[...] = lax.fori_loop(0, ntile, bwd_body, cb_s[...], unroll=min(SCAN_UNROLL, ntile))


def _lru(lx, conv_w, conv_b, wg, bg, lam, h0f, h0b, tt=512):
    b, n, wtot = lx.shape
    nh = wg.shape[1]
    hd = wtot // nh
    tt = min(tt, n)
    nt = n // tt
    hb_per = tt // HALO
    nhalo = n // HALO

    def main_spec(rev):
        return pl.BlockSpec((1, tt, hd), lambda bi, hi, ti: (bi, (nt - 1 - ti) if rev else ti, hi))

    def prev_spec(rev):
        return pl.BlockSpec(
            (1, HALO, hd),
            lambda bi, hi, ti: (bi, jnp.maximum(((nt - 1 - ti) if rev else ti) * hb_per - 1, 0), hi))

    def next_spec(rev):
        return pl.BlockSpec(
            (1, HALO, hd),
            lambda bi, hi, ti: (bi, jnp.minimum((((nt - 1 - ti) if rev else ti) + 1) * hb_per, nhalo - 1), hi))

    head_vec = lambda rows: pl.BlockSpec((rows, hd), lambda bi, hi, ti: (0, hi))
    state = pl.BlockSpec((1, 1, hd), lambda bi, hi, ti: (bi, 0, hi))
    bg2 = bg.reshape(2, nh * 2 * hd)
    return pl.pallas_call(
        _lru_kernel,
        grid=(b, nh, nt),
        in_specs=[
            main_spec(False), prev_spec(False), next_spec(False),
            main_spec(True), prev_spec(True), next_spec(True),
            head_vec(CONV_W), head_vec(1),
            pl.BlockSpec((2, 1, hd, 2 * hd), lambda bi, hi, ti: (0, hi, 0, 0)),
            pl.BlockSpec((2, 2 * hd), lambda bi, hi, ti: (0, hi)),
            head_vec(2), state, state,
        ],
        out_specs=[main_spec(False), main_spec(True)],
        out_shape=[jax.ShapeDtypeStruct((b, n, wtot), F32)] * 2,
        scratch_shapes=[pltpu.VMEM((tt + 2 * HALO, hd), F32), pltpu.VMEM((tt, hd), F32), pltpu.VMEM((tt, hd), F32),
                        pltpu.VMEM((1, hd), F32), pltpu.VMEM((1, hd), F32)],
        compiler_params=_params("parallel", "parallel", "arbitrary"),
        name="rglru",
    )(lx, lx, lx, lx, lx, lx, conv_w, conv_b.reshape(1, wtot), wg, bg2, lam,
      h0f.reshape(b, 1, wtot), h0b.reshape(b, 1, wtot))


def _mergeprep_kernel(of_ref, ob_ref, g_ref, hf_ref, hb_ref, ly_ref, o_ref):
    rw = of_ref.shape[1]
    for h in range(rw // RET_HD):
        sl = slice(h * RET_HD, (h + 1) * RET_HD)
        o = _rms(of_ref[:, sl] + ob_ref[:, sl])
        o_ref[:, sl] = (o * g_ref[:, sl].astype(F32)).astype(o_ref.dtype)
    o_ref[:, rw:] = (ly_ref[...].astype(F32) * (hf_ref[...] + hb_ref[...])).astype(o_ref.dtype)


def _mergeprep(of, ob, g, hf, hb, ly, tr=256):
    m, rw = of.shape
    lw = hf.shape[1]
    tr = min(tr, m)
    spec = lambda wd: pl.BlockSpec((tr, wd), lambda i: (i, 0))
    return pl.pallas_call(
        _mergeprep_kernel,
        grid=(m // tr,),
        in_specs=[spec(rw), spec(rw), spec(rw), spec(lw), spec(lw), spec(lw)],
        out_specs=spec(rw + lw),
        out_shape=jax.ShapeDtypeStruct((m, rw + lw), BF16),
        compiler_params=_params("parallel"),
        name="mergeprep",
    )(of, ob, g, hf, hb, ly)


def _merge_kernel(a_ref, wr_ref, wl_ref, gr_ref, gl_ref, o_ref):
    rw = wr_ref.shape[0]
    ret = jnp.dot(a_ref[:, :rw], wr_ref[...].astype(BF16), preferred_element_type=F32)
    lru = jnp.dot(a_ref[:, rw:], wl_ref[...].astype(BF16), preferred_element_type=F32)
    o_ref[...] = (gr_ref[...].astype(F32) * ret + gl_ref[...].astype(F32) * lru).astype(o_ref.dtype)


def _merge(a, w_ret_out, w_lru_out, gates, tm=1024, tn=512):
    m = a.shape[0]
    rw, d = w_ret_out.shape
    lw = w_lru_out.shape[0]
    tm = min(tm, m)
    tn = min(tn, d)
    nj = d // tn
    return pl.pallas_call(
        _merge_kernel,
        grid=(m // tm, nj),
        in_specs=[
            pl.BlockSpec((tm, rw + lw), lambda i, j: (i, 0)),
            pl.BlockSpec((rw, tn), lambda i, j: (0, j)),
            pl.BlockSpec((lw, tn), lambda i, j: (0, j)),
            pl.BlockSpec((tm, tn), lambda i, j: (i, j)),
            pl.BlockSpec((tm, tn), lambda i, j: (i, nj + j)),
        ],
        out_specs=pl.BlockSpec((tm, tn), lambda i, j: (i, j)),
        out_shape=jax.ShapeDtypeStruct((m, d), BF16),
        compiler_params=_params("parallel", "arbitrary"),
        name="merge",
    )(a, w_ret_out, w_lru_out, gates, gates)


def _post_mix_kernel(x_ref, mix_ref, gt_ref, g1_ref, g2_ref, sc_ref, sh_ref, x1_ref, h2_ref):
    x1 = x_ref[0] + gt_ref[0] * (_rms(mix_ref[0]) * g1_ref[...])
    x1_ref[0] = x1
    h2_ref[0] = (_rms(x1) * g2_ref[...]) * (1.0 + sc_ref[0]) + sh_ref[0]


def _post_mix(x, mix, gate, g_post, g_pre, scale, shift, tr=256):
    b, n, d = x.shape
    tr = min(tr, n)
    rows = pl.BlockSpec((1, tr, d), lambda i, j: (i, j, 0))
    vec = pl.BlockSpec((1, d), lambda i, j: (0, 0))
    bvec = pl.BlockSpec((1, 1, d), lambda i, j: (i, 0, 0))
    return pl.pallas_call(
        _post_mix_kernel,
        grid=(b, n // tr),
        in_specs=[rows, rows, bvec, vec, vec, bvec, bvec],
        out_specs=[rows, rows],
        out_shape=[jax.ShapeDtypeStruct((b, n, d), F32)] * 2,
        compiler_params=_params("parallel", "parallel"),
        name="post_mix",
    )(x, mix, gate.reshape(b, 1, d), g_post.reshape(1, d), g_pre.reshape(1, d),
      scale.reshape(b, 1, d), shift.reshape(b, 1, d))


def _logits_kernel(rw_ref, h_ref, o_ref):
    o_ref[0] = _dot_nt(rw_ref[...], h_ref[0].astype(BF16))


def _router_logits(h2, router_w_t, tn=512):
    b, n, d = h2.shape
    e = router_w_t.shape[0]
    tn = min(tn, n)
    return pl.pallas_call(
        _logits_kernel,
        grid=(b, n // tn),
        in_specs=[pl.BlockSpec((e, d), lambda i, j: (0, 0)),
                  pl.BlockSpec((1, tn, d), lambda i, j: (i, j, 0))],
        out_specs=pl.BlockSpec((1, e, tn), lambda i, j: (i, 0, j)),
        out_shape=jax.ShapeDtypeStruct((b, e, n), F32),
        compiler_params=_params("parallel", "parallel"),
        name="router_logits",
    )(router_w_t, h2)


def _lane_cumsum_excl(m):
    e, n = m.shape
    tri = (lax.broadcasted_iota(I32, (LANES, LANES), 0) < lax.broadcasted_iota(I32, (LANES, LANES), 1)).astype(BF16)
    carry = jnp.zeros((e, 1), F32)
    outs = []
    for c in range(n // LANES):
        blk = m[:, c * LANES:(c + 1) * LANES]
        outs.append(jnp.dot(blk.astype(BF16), tri, preferred_element_type=F32) + carry)
        carry = carry + jnp.sum(blk, axis=1, keepdims=True)
    return jnp.concatenate(outs, axis=1)


SEL_SLOTS = 256
SEL_TOKENS = 1024


def _sel_kernel(lg_ref, idx_ref, gate_ref, dst_ref, seg_ref, aff_s, pos_s, dst_s, *, cap):
    lg = lg_ref[0]
    ne, n = lg.shape
    ex = jnp.exp(lg - jnp.max(lg, axis=0, keepdims=True))
    aff = ex / jnp.sum(ex, axis=0, keepdims=True)
    bits = lax.bitcast_convert_type(aff, I32)

    thr = jnp.zeros((ne, 1), I32)
    for bit in range(30, -1, -1):
        cand = thr | (1 << bit)
        cnt = jnp.sum((bits >= cand).astype(I32), axis=1, keepdims=True)
        thr = jnp.where(cnt >= cap, cand, thr)

    above = bits > thr
    tied = (bits == thr).astype(F32)
    need = (cap - jnp.sum(above.astype(I32), axis=1, keepdims=True)).astype(F32)
    sel = jnp.where(above, 1.0, jnp.where(_lane_cumsum_excl(tied) < need, tied, 0.0))

    per_token = jnp.sum(sel, axis=0, keepdims=True)
    seg_lo = _lane_cumsum_excl(jnp.broadcast_to(per_token, (ne, n)))[0:1, :]
    seg_ref[0, 0:1, :] = seg_lo
    seg_ref[0, 1:2, :] = seg_lo + per_token
    taken = seg_lo
    for e in range(ne):
        dst_s[e:e + 1, :] = taken
        taken = taken + sel[e:e + 1, :]

    aff_s[...] = aff
    pos_s[...] = (_lane_cumsum_excl(sel) + 1.0) * sel

    ss = min(SEL_SLOTS, cap)
    st = min(SEL_TOKENS, n)
    lane = lax.broadcasted_iota(I32, (ss, LANES), 1)
    s_base = lax.broadcasted_iota(I32, (ss, st), 0).astype(F32) + 1.0
    t_base = lax.broadcasted_iota(I32, (1, st), 1).astype(F32)
    idx_ref[...] = jnp.zeros(idx_ref.shape, idx_ref.dtype)
    gate_ref[...] = jnp.zeros(gate_ref.shape, gate_ref.dtype)
    dst_ref[...] = jnp.zeros(dst_ref.shape, dst_ref.dtype)

    def fold(v):
        out = v[:, :LANES]
        for q in range(1, st // LANES):
            out = out + v[:, q * LANES:(q + 1) * LANES]
        return out

    def per_expert(e, _):
        for sb in range(cap // ss):
            sv = s_base + float(sb * ss)

            def per_chunk(c, carry):
                ia, ga, da = carry
                t0 = pl.multiple_of(c * st, st)
                hit = pos_s[pl.ds(e, 1), pl.ds(t0, st)] == sv
                tok = t_base + lax.convert_element_type(c * st, F32)
                ia = ia + fold(jnp.where(hit, tok, 0.0))
                ga = ga + fold(jnp.where(hit, aff_s[pl.ds(e, 1), pl.ds(t0, st)], 0.0))
                da = da + fold(jnp.where(hit, dst_s[pl.ds(e, 1), pl.ds(t0, st)], 0.0))
                return ia, ga, da

            zero = jnp.zeros((ss, LANES), F32)
            ia, ga, da = lax.fori_loop(0, n // st, per_chunk, (zero, zero, zero))
            rows = slice(sb * ss, (sb + 1) * ss)
            mine = lane == e
            for ref, acc in ((idx_ref, ia), (gate_ref, ga), (dst_ref, da)):
                col = jnp.sum(acc, axis=1, keepdims=True).astype(ref.dtype)
                ref[0, rows, :] = jnp.where(mine, col, ref[0, rows, :])
        return 0

    lax.fori_loop(0, ne, per_expert, 0)


def _expert_choice(logits, cap):
    b, ne, n = logits.shape
    slot = pl.BlockSpec((1, cap, LANES), lambda i: (i, 0, 0))
    return pl.pallas_call(
        functools.partial(_sel_kernel, cap=cap),
        grid=(b,),
        in_specs=[pl.BlockSpec((1, ne, n), lambda i: (i, 0, 0))],
        out_specs=[slot, slot, slot, pl.BlockSpec((1, 2, n), lambda i: (i, 0, 0))],
        out_shape=[jax.ShapeDtypeStruct((b, cap, LANES), I32), jax.ShapeDtypeStruct((b, cap, LANES), F32),
                   jax.ShapeDtypeStruct((b, cap, LANES), I32), jax.ShapeDtypeStruct((b, 2, n), F32)],
        scratch_shapes=[pltpu.VMEM((ne, n), F32)] * 3,
        compiler_params=_params("parallel"),
        name="expert_choice",
    )(logits)


def _row_copy(src, dst, src_row, dst_row, sem):
    return pltpu.make_async_copy(src.at[pl.ds(src_row, 1), :], dst.at[pl.ds(dst_row, 1), :], sem)


def _ffn_kernel(src_ref, dst_ref, h_hbm, w1_ref, w3_ref, w2_ref, gate_ref, y_hbm, xin, xbf, acc, gsem, psem, *, nf):
    tm = xbf.shape[0]
    per_step = tm // nf
    i = pl.program_id(0)
    f = pl.program_id(1)
    nt = pl.num_programs(0)
    cur = i % 2
    oth = 1 - cur

    def gather(tile, slot, r):
        return _row_copy(h_hbm, xin.at[slot], src_ref[tile * tm + r], r, gsem.at[slot])

    def put(tile, slot, r):
        return _row_copy(acc.at[slot], y_hbm, r, dst_ref[tile * tm + r], psem.at[slot])

    def for_rows(fn):
        def body(r, carry):
            fn(r)
            return carry
        lax.fori_loop(0, tm, body, 0)

    @pl.when(jnp.logical_and(i == 0, f == 0))
    def _():
        for_rows(lambda r: gather(0, 0, r).start())

    @pl.when(f == 0)
    def _():
        for_rows(lambda r: _row_copy(h_hbm, xin.at[cur], 0, r, gsem.at[cur]).wait())
        xbf[...] = xin[cur].astype(BF16)
        acc[cur] = jnp.zeros((tm, acc.shape[2]), F32)

    @pl.when(i + 1 < nt)
    def _():
        for r in range(per_step):
            gather(i + 1, oth, f * per_step + r).start()

    @pl.when(i > 0)
    def _():
        for r in range(per_step):
            put(i - 1, oth, f * per_step + r).start()

    x = xbf[...]
    h1 = jnp.dot(x, w1_ref[0], preferred_element_type=F32)
    h3 = jnp.dot(x, w3_ref[0], preferred_element_type=F32)
    hid = (h1 * _sigmoid(h1) * h3).astype(BF16)
    acc[cur] += jnp.dot(hid, w2_ref[0], preferred_element_type=F32)

    @pl.when(f == nf - 1)
    def _():
        acc[cur] = acc[cur] * gate_ref[...]

        @pl.when(i > 0)
        def _():
            for_rows(lambda r: _row_copy(acc.at[oth], y_hbm, r, 0, psem.at[oth]).wait())

        @pl.when(i == nt - 1)
        def _():
            for_rows(lambda r: put(i, cur, r).start())
            for_rows(lambda r: _row_copy(acc.at[cur], y_hbm, r, 0, psem.at[cur]).wait())


def _expert_ffn(h2, w1, w3, w2, src_rows, dst_rows, gate, tm=512, tf=256):
    ne, d, ff = w1.shape
    total = src_rows.shape[0]
    s = total // ne
    tm = min(tm, s)
    tf = min(tf, ff)
    nf = ff // tf
    per_e = s // tm
    return pl.pallas_call(
        functools.partial(_ffn_kernel, nf=nf),
        grid_spec=pltpu.PrefetchScalarGridSpec(
            num_scalar_prefetch=2,
            grid=(ne * per_e, nf),
            in_specs=[
                pl.BlockSpec(memory_space=pl.ANY),
                pl.BlockSpec((1, d, tf), lambda i, f, sr, dr: (i // per_e, 0, f)),
                pl.BlockSpec((1, d, tf), lambda i, f, sr, dr: (i // per_e, 0, f)),
                pl.BlockSpec((1, tf, d), lambda i, f, sr, dr: (i // per_e, f, 0)),
                pl.BlockSpec((tm, 1), lambda i, f, sr, dr: (i, 0)),
            ],
            out_specs=pl.BlockSpec(memory_space=pl.ANY),
            scratch_shapes=[pltpu.VMEM((2, tm, d), F32), pltpu.VMEM((tm, d), BF16), pltpu.VMEM((2, tm, d), F32),
                            pltpu.SemaphoreType.DMA((2,)), pltpu.SemaphoreType.DMA((2,))],
        ),
        out_shape=jax.ShapeDtypeStruct((total, d), F32),
        compiler_params=_params("arbitrary", "arbitrary"),
        name="expert_ffn",
    )(src_rows, dst_rows, h2, w1, w3, w2, gate)


SEG_ROWS = 256


def _combine_kernel(first_ref, last_ref, x_ref, lo_ref, hi_ref, gt_ref, g_ref, y_hbm, o_ref, buf, acc, sem, *,
                    rows_per_sample):
    b = pl.program_id(0)
    tile = b * pl.num_programs(1) + pl.program_id(1)
    first = first_ref[tile]
    last = last_ref[tile]
    r0 = (first // SUBLANES) * SUBLANES
    nch = jnp.where(last > first, (last - r0 + SEG_ROWS - 1) // SEG_ROWS, 0)

    def chunk_start(c):
        return pl.multiple_of(jnp.minimum(r0 + c * SEG_ROWS, rows_per_sample - SEG_ROWS), SUBLANES)

    def chunk_copy(c, slot):
        src = pl.multiple_of(b * rows_per_sample + chunk_start(c), SUBLANES)
        return pltpu.make_async_copy(y_hbm.at[pl.ds(src, SEG_ROWS), :], buf.at[slot], sem.at[slot])

    acc[...] = jnp.zeros(acc.shape, acc.dtype)

    @pl.when(nch > 0)
    def _():
        chunk_copy(0, 0).start()

    lo = lo_ref[0]
    hi = hi_ref[0]

    def body(c, carry):
        slot = c % 2

        @pl.when(c + 1 < nch)
        def _():
            chunk_copy(c + 1, 1 - slot).start()

        chunk_copy(c, slot).wait()
        row = (lax.broadcasted_iota(I32, (1, SEG_ROWS), 1) + chunk_start(c)).astype(F32)
        nominal = (r0 + c * SEG_ROWS).astype(F32)
        pick = jnp.logical_and(jnp.logical_and(lo <= row, row < hi), row >= nominal)
        pick = jnp.where(pick, 1.0, 0.0).astype(BF16)
        y = buf[slot]
        y_hi = y.astype(BF16)
        y_lo = (y - y_hi.astype(F32)).astype(BF16)
        acc[...] += (jnp.dot(pick, y_hi, preferred_element_type=F32)
                     + jnp.dot(pick, y_lo, preferred_element_type=F32))
        return carry

    lax.fori_loop(0, nch, body, 0)
    o_ref[0] = x_ref[0] + gt_ref[0] * (_rms(acc[...]) * g_ref[...])


def _combine(x1, y, seg, gate, g, tt=256):
    b, n, d = x1.shape
    rows_per_sample = y.shape[0] // b
    tt = min(tt, n)
    nj = n // tt
    seg_i = seg.astype(I32)
    first = seg_i[:, 0, ::tt].reshape(-1)
    last = seg_i[:, 1, tt - 1::tt].reshape(-1)
    rows = pl.BlockSpec((1, tt, d), lambda i, j, fr, lr: (i, j, 0))
    col = pl.BlockSpec((1, tt, 1), lambda i, j, fr, lr: (i, j, 0))
    return pl.pallas_call(
        functools.partial(_combine_kernel, rows_per_sample=rows_per_sample),
        grid_spec=pltpu.PrefetchScalarGridSpec(
            num_scalar_prefetch=2,
            grid=(b, nj),
            in_specs=[rows, col, col,
                      pl.BlockSpec((1, 1, d), lambda i, j, fr, lr: (i, 0, 0)),
                      pl.BlockSpec((1, d), lambda i, j, fr, lr: (0, 0)),
                      pl.BlockSpec(memory_space=pl.ANY)],
            out_specs=rows,
            scratch_shapes=[pltpu.VMEM((2, SEG_ROWS, d), F32), pltpu.VMEM((tt, d), F32),
                            pltpu.SemaphoreType.DMA((2,))],
        ),
        out_shape=jax.ShapeDtypeStruct((b, n, d), F32),
        compiler_params=_params("arbitrary", "arbitrary"),
        name="combine_residual",
    )(first, last, x1, seg[:, 0, :, None], seg[:, 1, :, None], gate.reshape(b, 1, d), g.reshape(1, d), y)


def _layer(x, c, ctx, c_ctx, p):
    b, n, d = x.shape
    lc = ctx.shape[1]
    rw = RET_HEADS * RET_HD
    lw = p["conv_w"].shape[1]
    q_off, k_off, v_off, g_off = 0, rw, 2 * rw, 3 * rw
    lx_off = 4 * rw
    ly_off = lx_off + lw
    gr_off = ly_off + lw

    cvecs = jnp.zeros((SUBLANES, d), F32).at[:b].set(c).at[b].set(c_ctx)
    mod = _ada_mod(cvecs, p["ada_w"], p["ada_b"])
    sh1, sc1, gt1, sh2, sc2, gt2 = [mod[:b, i * d:(i + 1) * d] for i in range(6)]
    csh1 = jnp.broadcast_to(mod[b:b + 1, :d], (b, d))
    csc1 = jnp.broadcast_to(mod[b:b + 1, d:2 * d], (b, d))

    w_in = p["w_in"]
    h = _prenorm(x, p["n_pre_mix"], sc1, sh1).reshape(b * n, d)
    hc = _prenorm(ctx, p["n_pre_mix"], csc1, csh1).reshape(b * lc, d)

    k_scale = RET_HD ** -0.5
    q_tabs, k_tabs = _rope_tables(n, k_scale)
    q = _proj(h, w_in, q_off, rw, "rope", tables=q_tabs, rows_per_seq=n)
    k = _proj(h, w_in, k_off, rw, "rope", tables=k_tabs, rows_per_seq=n)
    v = _proj(h, w_in, v_off, rw)
    g = _proj(h, w_in, g_off, rw, "silu")
    lx = _proj(h, w_in, lx_off, lw, out_dtype=F32)
    ly = _proj(h, w_in, ly_off, lw, "gelu")
    gates = _proj(h, w_in, gr_off, 2 * d, "sigmoid")
    kc = _proj(hc, w_in, k_off, rw, "scale", scale=k_scale)
    vc = _proj(hc, w_in, v_off, rw)
    lxc = _proj(hc, w_in, lx_off, lw, out_dtype=F32)

    of, ob = _retention(q.reshape(b, n, rw), k.reshape(b, n, rw), v.reshape(b, n, rw),
                        kc.reshape(b, lc, rw), vc.reshape(b, lc, rw), p["ret_lg"])

    wg = jnp.concatenate([p["gate_a_w"], p["gate_x_w"]], axis=-1).astype(BF16)
    hd = lw // LRU_HEADS
    bg = jnp.concatenate([p["gate_a_b"].reshape(2, LRU_HEADS, hd), p["gate_x_b"].reshape(2, LRU_HEADS, hd)], axis=-1)
    lru_w = (p["conv_w"], p["conv_b"], wg, bg, p["lam"])
    zero_state = jnp.zeros((b, lw), F32)
    hfc, hbc = _lru(lxc.reshape(b, lc, lw), *lru_w, zero_state, zero_state)
    hf, hb = _lru(lx.reshape(b, n, lw), *lru_w, hfc[:, -1], hbc[:, 0])

    a = _mergeprep(of.reshape(b * n, rw), ob.reshape(b * n, rw), g,
                   hf.reshape(b * n, lw), hb.reshape(b * n, lw), ly)
    merged = _merge(a, p["w_ret_out"], p["w_lru_out"], gates)
    mix = _proj(merged, p["w_out"], 0, d, out_dtype=F32)
    x1, h2 = _post_mix(x, mix.reshape(b, n, d), gt1, p["n_post_mix"], p["n_pre_ffn"], sc2, sh2)

    ne = p["router_w"].shape[1]
    cap = EC_CAPACITY * n // ne
    logits = _router_logits(h2, p["router_w"].T.astype(BF16))
    idx, gate, dst, seg = _expert_choice(logits, cap)
    by_expert = lambda a: jnp.transpose(a[:, :, :ne], (2, 0, 1))
    sample = jnp.arange(b, dtype=I32)[None, :, None]
    src_rows = (by_expert(idx) + sample * n).reshape(-1)
    dst_rows = (by_expert(dst) + sample * (ne * cap)).reshape(-1)
    y = _expert_ffn(h2.reshape(b * n, d), p["w1"].astype(BF16), p["w3"].astype(BF16), p["w2"].astype(BF16),
                    src_rows, dst_rows, by_expert(gate).reshape(-1, 1))
    return _combine(x1, y, seg, gt2, p["n_post_ffn"])


def kernel(x, c, ctx, c_ctx, ada_w, ada_b, norm_pre_mix, norm_post_mix, norm_pre_ffn, norm_post_ffn, w_in, ret_log_gamma, w_ret_out, lru_conv_w, lru_conv_b, lru_gate_a_w, lru_gate_a_b, lru_gate_x_w, lru_gate_x_b, lru_lambda, w_lru_out, w_out, router_w, expert_w1, expert_w3, expert_w2):
    depth = ada_w.shape[0]
    assert depth == 1, "context-stream update between layers is not implemented"
    l = 0
    p = {
        "ada_w": ada_w[l], "ada_b": ada_b[l],
        "n_pre_mix": norm_pre_mix[l], "n_post_mix": norm_post_mix[l],
        "n_pre_ffn": norm_pre_ffn[l], "n_post_ffn": norm_post_ffn[l],
        "w_in": w_in[l], "ret_lg": ret_log_gamma[l], "w_ret_out": w_ret_out[l],
        "conv_w": lru_conv_w[l], "conv_b": lru_conv_b[l],
        "gate_a_w": lru_gate_a_w[l], "gate_a_b": lru_gate_a_b[l],
        "gate_x_w": lru_gate_x_w[l], "gate_x_b": lru_gate_x_b[l], "lam": lru_lambda[l],
        "w_lru_out": w_lru_out[l], "w_out": w_out[l], "router_w": router_w[l],
        "w1": expert_w1[l], "w3": expert_w3[l], "w2": expert_w2[l],
    }
    return _layer(x, c, ctx, c_ctx, p)
```

```python
import functools

import jax
import jax.numpy as jnp
from jax import lax
from jax.experimental import pallas as pl
from jax.experimental.pallas import tpu as pltpu

F32 = jnp.float32
BF16 = jnp.bfloat16
I32 = jnp.int32

EPS = 1e-6
GRID_W = 64
RET_HEADS = 8
RET_HD = 256
ROPE_BASE = 10000.0
LRU_HEADS = 8
LRU_C = 8.0
CONV_W = 4
CONV_LEFT = 2
N_EXPERTS = 16
EC_CAPACITY = 2

LANES = 128
SUBLANES = 8
VMEM_LIMIT_BYTES = 60 * 1024 * 1024


def _params(*sem):
    return pltpu.CompilerParams(dimension_semantics=sem, vmem_limit_bytes=VMEM_LIMIT_BYTES)


def _rms(x):
    return x * lax.rsqrt(jnp.mean(x * x, axis=-1, keepdims=True) + EPS)


def _sigmoid(x):
    return 0.5 * jnp.tanh(0.5 * x) + 0.5


U32 = jnp.uint32


def _pack_bf16_pairs(x):
    c = x.shape[1] // 2
    lo = lax.bitcast_convert_type(x[:, :c].astype(BF16).astype(F32), U32)
    hi = lax.bitcast_convert_type(x[:, c:].astype(BF16).astype(F32), U32)
    return lax.shift_right_logical(lo, jnp.full(lo.shape, 16, U32)) | (hi & jnp.uint32(0xFFFF0000))


def _unpack_bf16_pairs(w):
    lo = lax.bitcast_convert_type(lax.shift_left(w, jnp.full(w.shape, 16, U32)), F32)
    hi = lax.bitcast_convert_type(w & jnp.uint32(0xFFFF0000), F32)
    return lo.astype(BF16), hi.astype(BF16)


def _gelu_tanh(x):
    return 0.5 * x * (1.0 + jnp.tanh(0.7978845608028654 * (x + 0.044715 * (x * x * x))))


def _ada_kernel(c_ref, w_ref, b_ref, o_ref):
    c = c_ref[...]
    s = (c * _sigmoid(c)).astype(BF16)
    o_ref[...] = jnp.dot(s, w_ref[...].astype(BF16), preferred_element_type=F32) + b_ref[...]


def _ada_mod(cvecs, ada_w, ada_b, tn=512):
    d, w = ada_w.shape
    tn = min(tn, w)
    return pl.pallas_call(
        _ada_kernel,
        grid=(w // tn,),
        in_specs=[
            pl.BlockSpec((SUBLANES, d), lambda j: (0, 0)),
            pl.BlockSpec((d, tn), lambda j: (0, j)),
            pl.BlockSpec((1, tn), lambda j: (0, j)),
        ],
        out_specs=pl.BlockSpec((SUBLANES, tn), lambda j: (0, j)),
        out_shape=jax.ShapeDtypeStruct((SUBLANES, w), F32),
        compiler_params=_params("parallel"),
        name="ada_mod",
    )(cvecs, ada_w, ada_b.reshape(1, w))


def _prenorm_kernel(x_ref, g_ref, sc_ref, sh_ref, o_ref):
    y = _rms(x_ref[0]) * g_ref[...]
    o_ref[0] = (y * (1.0 + sc_ref[0]) + sh_ref[0]).astype(o_ref.dtype)


def _prenorm(x, g, scale, shift, tr=256):
    b, n, d = x.shape
    tr = min(tr, n)
    return pl.pallas_call(
        _prenorm_kernel,
        grid=(b, n // tr),
        in_specs=[
            pl.BlockSpec((1, tr, d), lambda i, j: (i, j, 0)),
            pl.BlockSpec((1, d), lambda i, j: (0, 0)),
            pl.BlockSpec((1, 1, d), lambda i, j: (i, 0, 0)),
            pl.BlockSpec((1, 1, d), lambda i, j: (i, 0, 0)),
        ],
        out_specs=pl.BlockSpec((1, tr, d), lambda i, j: (i, j, 0)),
        out_shape=jax.ShapeDtypeStruct((b, n, d), BF16),
        compiler_params=_params("parallel", "parallel"),
        name="prenorm",
    )(x, g.reshape(1, d), scale.reshape(b, 1, d), shift.reshape(b, 1, d))


def _proj_kernel(*refs, epilogue, scale):
    if epilogue == "rope":
        h_ref, w_ref, cos_ref, sin_ref, o_ref, wbf = refs
    else:
        h_ref, w_ref, o_ref, wbf = refs

    @pl.when(pl.program_id(1) == 0)
    def _():
        wbf[...] = w_ref[...].astype(BF16)

    acc = jnp.dot(h_ref[...], wbf[...], preferred_element_type=F32)
    if epilogue == "rope":
        for s in range(acc.shape[1] // LANES):
            a = acc[:, s * LANES:(s + 1) * LANES]
            ts = (s * LANES) % RET_HD
            cos = cos_ref[:, ts:ts + LANES]
            sin = sin_ref[:, ts:ts + LANES]
            o_ref[:, s * LANES:(s + 1) * LANES] = (
                a * cos + pltpu.roll(a, LANES // 2, 1) * sin).astype(o_ref.dtype)
        return
    if epilogue == "silu":
        acc = acc * _sigmoid(acc)
    elif epilogue == "gelu":
        acc = _gelu_tanh(acc)
    elif epilogue == "sigmoid":
        acc = _sigmoid(acc)
    elif epilogue == "scale":
        acc = acc * scale
    o_ref[...] = acc.astype(o_ref.dtype)


def _proj(h, w, col_off, width, epilogue="none", out_dtype=BF16, tables=None, rows_per_seq=None,
          scale=1.0, tm=1024, tn=512):
    m, k = h.shape
    tm = min(tm, m)
    tn = min(tn, width)
    if rows_per_seq is not None:
        tm = min(tm, rows_per_seq)
    jo = col_off // tn
    in_specs = [
        pl.BlockSpec((tm, k), lambda j, i: (i, 0)),
        pl.BlockSpec((k, tn), lambda j, i: (0, jo + j)),
    ]
    args = [h, w]
    if epilogue == "rope":
        nblk = rows_per_seq // tm
        tspec = pl.BlockSpec((tm, RET_HD), lambda j, i: (i % nblk, 0))
        in_specs += [tspec, tspec]
        args += list(tables)
    return pl.pallas_call(
        functools.partial(_proj_kernel, epilogue=epilogue, scale=scale),
        grid=(width // tn, m // tm),
        in_specs=in_specs,
        out_specs=pl.BlockSpec((tm, tn), lambda j, i: (i, j)),
        out_shape=jax.ShapeDtypeStruct((m, width), out_dtype),
        scratch_shapes=[pltpu.VMEM((k, tn), BF16)],
        compiler_params=_params("parallel", "arbitrary"),
        name="proj_" + epilogue,
    )(*args)


def _rope_tables(n, k_scale):
    quarter = RET_HD // 4
    freqs = ROPE_BASE ** (-jnp.arange(quarter, dtype=F32) / quarter)
    rows = n // GRID_W
    row = jnp.repeat(jnp.arange(rows, dtype=F32), GRID_W)
    col = jnp.tile(jnp.arange(GRID_W, dtype=F32), rows)

    def part(pos):
        ang = pos[:, None] * freqs[None, :]
        c, s = jnp.cos(ang), jnp.sin(ang)
        return jnp.concatenate([c, c], axis=-1), jnp.concatenate([-s, s], axis=-1)

    cr, sr = part(row)
    cc, sc = part(col)
    cos = jnp.concatenate([cr, cc], axis=-1)
    sin = jnp.concatenate([sr, sc], axis=-1)
    return (cos, sin), (cos * k_scale, sin * k_scale)


def _dot_nt(a, b):
    return lax.dot_general(a, b, (((1,), (1,)), ((), ())), preferred_element_type=F32)


def _dot_tn(a, b):
    return lax.dot_general(a, b, (((0,), (0,)), ((), ())), preferred_element_type=F32)


def _ret_kernel(lg_ref, qf_ref, kf_ref, vf_ref, qb_ref, kb_ref, vb_ref, kc_ref, vc_ref,
                of_ref, ob_ref, sf_ref, sb_ref, *, csz):
    h = pl.program_id(1)
    c = pl.program_id(2)
    lgf = lg_ref[0, h]
    lgb = lg_ref[1, h]
    nsub = qf_ref.shape[1] // csz

    @pl.when(c == 0)
    def _():
        kc = kc_ref[0].astype(F32)
        vc = vc_ref[0]
        lc = kc.shape[0]
        m = lax.broadcasted_iota(I32, (lc, 1), 0).astype(F32)
        sf_ref[...] = _dot_tn((kc * jnp.exp((lc - 1.0 - m) * lgf)).astype(BF16), vc)
        sb_ref[...] = _dot_tn((kc * jnp.exp(m * lgb)).astype(BF16), vc)

    qi = lax.broadcasted_iota(I32, (csz, csz), 0)
    ki = lax.broadcasted_iota(I32, (csz, csz), 1)
    j = lax.broadcasted_iota(I32, (csz, 1), 0).astype(F32)

    def direction(q_ref, k_ref, v_ref, o_ref, s_ref, lg, dist, q_pow, k_pow, order):
        intra = jnp.where(dist >= 0, jnp.exp(jnp.maximum(dist, 0).astype(F32) * lg), 0.0)
        q_dec = jnp.exp(q_pow * lg)
        k_dec = jnp.exp(k_pow * lg)
        chunk_dec = jnp.exp(jnp.full((1, 1), csz, F32) * lg)
        state = s_ref[...]
        for sub in order:
            rows = slice(sub * csz, (sub + 1) * csz)
            q = q_ref[0, rows, :]
            k = k_ref[0, rows, :]
            v = v_ref[0, rows, :]
            s = _dot_nt(q, k) * intra
            o = jnp.dot(s.astype(BF16), v, preferred_element_type=F32)
            o_ref[0, rows, :] = o + jnp.dot(q, state.astype(BF16), preferred_element_type=F32) * q_dec
            kd = (k.astype(F32) * k_dec).astype(BF16)
            state = state * chunk_dec + _dot_tn(kd, v)
        s_ref[...] = state

    direction(qf_ref, kf_ref, vf_ref, of_ref, sf_ref, lgf, qi - ki, j + 1.0, csz - 1.0 - j, range(nsub))
    direction(qb_ref, kb_ref, vb_ref, ob_ref, sb_ref, lgb, ki - qi, csz - j, j, range(nsub - 1, -1, -1))


def _retention(q, k, v, kc, vc, log_gamma, chunk=256, block=1024):
    b, n, hw = q.shape
    dh = RET_HD
    nh = hw // dh
    lc = kc.shape[1]
    chunk = min(chunk, n)
    block = min(block, n)
    nc = n // block
    fwd = pl.BlockSpec((1, block, dh), lambda bi, hi, ci: (bi, ci, hi))
    bwd = pl.BlockSpec((1, block, dh), lambda bi, hi, ci: (bi, nc - 1 - ci, hi))
    ctx = pl.BlockSpec((1, lc, dh), lambda bi, hi, ci: (bi, 0, hi))
    return pl.pallas_call(
        functools.partial(_ret_kernel, csz=chunk),
        grid=(b, nh, nc),
        in_specs=[pl.BlockSpec(memory_space=pltpu.SMEM), fwd, fwd, fwd, bwd, bwd, bwd, ctx, ctx],
        out_specs=[fwd, bwd],
        out_shape=[jax.ShapeDtypeStruct((b, n, hw), F32)] * 2,
        scratch_shapes=[pltpu.VMEM((dh, dh), F32), pltpu.VMEM((dh, dh), F32)],
        compiler_params=_params("parallel", "parallel", "arbitrary"),
        name="retention",
    )(log_gamma.astype(F32), q, k, v, q, k, v, kc, vc)


HALO = SUBLANES
SCAN_UNROLL = 8


def _lru_kernel(mf_ref, pf_ref, nf_ref, mb_ref, pb_ref, nb_ref, cw_ref, cb_ref, wg_ref, bg_ref, lam_ref,
                h0f_ref, h0b_ref, hf_ref, hb_ref, ext_s, a_s, b_s, cf_s, cb_s):
    t = pl.program_id(2)
    nt = pl.num_programs(2)
    tt = mf_ref.shape[1]
    w = mf_ref.shape[2]

    @pl.when(t == 0)
    def _():
        cf_s[...] = h0f_ref[0]
        cb_s[...] = h0b_ref[0]

    def coeffs(m_ref, p_ref, n_ref, has_prev, has_next, d):
        ext_s[0:HALO, :] = jnp.where(has_prev, p_ref[0], 0.0)
        ext_s[HALO:HALO + tt, :] = m_ref[0]
        ext_s[HALO + tt:, :] = jnp.where(has_next, n_ref[0], 0.0)
        u = cb_ref[...]
        for i in range(CONV_W):
            off = HALO + i - CONV_LEFT
            u = u + ext_s[off:off + tt, :] * cw_ref[i:i + 1, :]
        g = jnp.dot(u.astype(BF16), wg_ref[d, 0], preferred_element_type=F32) + bg_ref[d:d + 1, :]
        r = _sigmoid(g[:, :w])
        ig = _sigmoid(g[:, w:])
        lam = lam_ref[d:d + 1, :]
        softplus = jnp.maximum(-lam, 0.0) + jnp.log(1.0 + jnp.exp(-jnp.abs(lam)))
        log_a = -LRU_C * r * softplus
        a = jnp.exp(log_a)
        a_s[...] = a
        b_s[...] = jnp.sqrt(-jnp.tanh(log_a) * (a * a + 1.0)) * (ig * u)

    row = lax.broadcasted_iota(I32, (SUBLANES, w), 0)
    ntile = tt // SUBLANES

    coeffs(mf_ref, pf_ref, nf_ref, t > 0, t < nt - 1, 0)

    def fwd_body(i, carry):
        r0 = pl.multiple_of(i * SUBLANES, SUBLANES)
        a = a_s[pl.ds(r0, SUBLANES), :]
        b = b_s[pl.ds(r0, SUBLANES), :]
        for s in (1, 2, 4):
            keep = row >= s
            b = jnp.where(keep, a * pltpu.roll(b, s, 0) + b, b)
            a = jnp.where(keep, a * pltpu.roll(a, s, 0), a)
        hcur = a * carry + b
        hf_ref[0, pl.ds(r0, SUBLANES), :] = hcur
        return hcur[SUBLANES - 1:SUBLANES, :]

    cf_s[...] = lax.fori_loop(0, ntile, fwd_body, cf_s[...], unroll=min(SCAN_UNROLL, ntile))

    coeffs(mb_ref, pb_ref, nb_ref, t < nt - 1, t > 0, 1)

    def bwd_body(i, carry):
        r0 = pl.multiple_of((ntile - 1 - i) * SUBLANES, SUBLANES)
        a = a_s[pl.ds(r0, SUBLANES), :]
        b = b_s[pl.ds(r0, SUBLANES), :]
        for s in (1, 2, 4):
            keep = row < SUBLANES - s
            b = jnp.where(keep, a * pltpu.roll(b, SUBLANES - s, 0) + b, b)
            a = jnp.where(keep, a * pltpu.roll(a, SUBLANES - s, 0), a)
        hcur = a * carry + b
        hb_ref[0, pl.ds(r0, SUBLANES), :] = hcur
        return hcur[0:1, :]

    cb_s[...] = lax.fori_loop(0, ntile, bwd_body, cb_s[...], unroll=min(SCAN_UNROLL, ntile))


def _lru(lx, conv_w, conv_b, wg, bg, lam, h0f, h0b, tt=512):
    b, n, wtot = lx.shape
    nh = wg.shape[1]
    hd = wtot // nh
    tt = min(tt, n)
    nt = n // tt
    hb_per = tt // HALO
    nhalo = n // HALO

    def main_spec(rev):
        return pl.BlockSpec((1, tt, hd), lambda bi, hi, ti: (bi, (nt - 1 - ti) if rev else ti, hi))

    def prev_spec(rev):
        return pl.BlockSpec(
            (1, HALO, hd),
            lambda bi, hi, ti: (bi, jnp.maximum(((nt - 1 - ti) if rev else ti) * hb_per - 1, 0), hi))

    def next_spec(rev):
        return pl.BlockSpec(
            (1, HALO, hd),
            lambda bi, hi, ti: (bi, jnp.minimum((((nt - 1 - ti) if rev else ti) + 1) * hb_per, nhalo - 1), hi))

    head_vec = lambda rows: pl.BlockSpec((rows, hd), lambda bi, hi, ti: (0, hi))
    state = pl.BlockSpec((1, 1, hd), lambda bi, hi, ti: (bi, 0, hi))
    bg2 = bg.reshape(2, nh * 2 * hd)
    return pl.pallas_call(
        _lru_kernel,
        grid=(b, nh, nt),
        in_specs=[
            main_spec(False), prev_spec(False), next_spec(False),
            main_spec(True), prev_spec(True), next_spec(True),
            head_vec(CONV_W), head_vec(1),
            pl.BlockSpec((2, 1, hd, 2 * hd), lambda bi, hi, ti: (0, hi, 0, 0)),
            pl.BlockSpec((2, 2 * hd), lambda bi, hi, ti: (0, hi)),
            head_vec(2), state, state,
        ],
        out_specs=[main_spec(False), main_spec(True)],
        out_shape=[jax.ShapeDtypeStruct((b, n, wtot), F32)] * 2,
        scratch_shapes=[pltpu.VMEM((tt + 2 * HALO, hd), F32), pltpu.VMEM((tt, hd), F32), pltpu.VMEM((tt, hd), F32),
                        pltpu.VMEM((1, hd), F32), pltpu.VMEM((1, hd), F32)],
        compiler_params=_params("parallel", "parallel", "arbitrary"),
        name="rglru",
    )(lx, lx, lx, lx, lx, lx, conv_w, conv_b.reshape(1, wtot), wg, bg2, lam,
      h0f.reshape(b, 1, wtot), h0b.reshape(b, 1, wtot))


def _mergeprep_kernel(of_ref, ob_ref, g_ref, hf_ref, hb_ref, ly_ref, o_ref):
    rw = of_ref.shape[1]
    for h in range(rw // RET_HD):
        sl = slice(h * RET_HD, (h + 1) * RET_HD)
        o = _rms(of_ref[:, sl] + ob_ref[:, sl])
        o_ref[:, sl] = (o * g_ref[:, sl].astype(F32)).astype(o_ref.dtype)
    o_ref[:, rw:] = (ly_ref[...].astype(F32) * (hf_ref[...] + hb_ref[...])).astype(o_ref.dtype)


def _mergeprep(of, ob, g, hf, hb, ly, tr=256):
    m, rw = of.shape
    lw = hf.shape[1]
    tr = min(tr, m)
    spec = lambda wd: pl.BlockSpec((tr, wd), lambda i: (i, 0))
    return pl.pallas_call(
        _mergeprep_kernel,
        grid=(m // tr,),
        in_specs=[spec(rw), spec(rw), spec(rw), spec(lw), spec(lw), spec(lw)],
        out_specs=spec(rw + lw),
        out_shape=jax.ShapeDtypeStruct((m, rw + lw), BF16),
        compiler_params=_params("parallel"),
        name="mergeprep",
    )(of, ob, g, hf, hb, ly)


def _merge_kernel(a_ref, wr_ref, wl_ref, gr_ref, gl_ref, o_ref, wr_bf, wl_bf):
    rw = wr_ref.shape[0]

    @pl.when(pl.program_id(1) == 0)
    def _():
        wr_bf[...] = wr_ref[...].astype(BF16)
        wl_bf[...] = wl_ref[...].astype(BF16)

    ret = jnp.dot(a_ref[:, :rw], wr_bf[...], preferred_element_type=F32)
    lru = jnp.dot(a_ref[:, rw:], wl_bf[...], preferred_element_type=F32)
    o_ref[...] = (gr_ref[...].astype(F32) * ret + gl_ref[...].astype(F32) * lru).astype(o_ref.dtype)


def _merge(a, w_ret_out, w_lru_out, gates, tm=1024, tn=512):
    m = a.shape[0]
    rw, d = w_ret_out.shape
    lw = w_lru_out.shape[0]
    tm = min(tm, m)
    tn = min(tn, d)
    nj = d // tn
    return pl.pallas_call(
        _merge_kernel,
        grid=(nj, m // tm),
        in_specs=[
            pl.BlockSpec((tm, rw + lw), lambda j, i: (i, 0)),
            pl.BlockSpec((rw, tn), lambda j, i: (0, j)),
            pl.BlockSpec((lw, tn), lambda j, i: (0, j)),
            pl.BlockSpec((tm, tn), lambda j, i: (i, j)),
            pl.BlockSpec((tm, tn), lambda j, i: (i, nj + j)),
        ],
        out_specs=pl.BlockSpec((tm, tn), lambda j, i: (i, j)),
        out_shape=jax.ShapeDtypeStruct((m, d), BF16),
        scratch_shapes=[pltpu.VMEM((rw, tn), BF16), pltpu.VMEM((lw, tn), BF16)],
        compiler_params=_params("parallel", "arbitrary"),
        name="merge",
    )(a, w_ret_out, w_lru_out, gates, gates)


def _post_mix_kernel(x_ref, mix_ref, gt_ref, g1_ref, g2_ref, sc_ref, sh_ref, rw_ref, x1_ref, h2_ref, lg_ref):
    x1 = x_ref[0] + gt_ref[0] * (_rms(mix_ref[0]) * g1_ref[...])
    x1_ref[0] = x1
    h2 = (_rms(x1) * g2_ref[...]) * (1.0 + sc_ref[0]) + sh_ref[0]
    h2_ref[0] = _pack_bf16_pairs(h2)
    lg_ref[0] = _dot_nt(rw_ref[...], h2.astype(BF16))


def _post_mix(x, mix, gate, g_post, g_pre, scale, shift, router_w_t, tr=256):
    b, n, d = x.shape
    e = router_w_t.shape[0]
    tr = min(tr, n)
    rows = pl.BlockSpec((1, tr, d), lambda i, j: (i, j, 0))
    vec = pl.BlockSpec((1, d), lambda i, j: (0, 0))
    bvec = pl.BlockSpec((1, 1, d), lambda i, j: (i, 0, 0))
    return pl.pallas_call(
        _post_mix_kernel,
        grid=(b, n // tr),
        in_specs=[rows, rows, bvec, vec, vec, bvec, bvec, pl.BlockSpec((e, d), lambda i, j: (0, 0))],
        out_specs=[rows, pl.BlockSpec((1, tr, d // 2), lambda i, j: (i, j, 0)),
                   pl.BlockSpec((1, e, tr), lambda i, j: (i, 0, j))],
        out_shape=[jax.ShapeDtypeStruct((b, n, d), F32), jax.ShapeDtypeStruct((b, n, d // 2), U32),
                   jax.ShapeDtypeStruct((b, e, n), F32)],
        compiler_params=_params("parallel", "parallel"),
        name="post_mix",
    )(x, mix, gate.reshape(b, 1, d), g_post.reshape(1, d), g_pre.reshape(1, d),
      scale.reshape(b, 1, d), shift.reshape(b, 1, d), router_w_t)


def _lane_cumsum_excl(m):
    e, n = m.shape
    tri = (lax.broadcasted_iota(I32, (LANES, LANES), 0) < lax.broadcasted_iota(I32, (LANES, LANES), 1)).astype(BF16)
    carry = jnp.zeros((e, 1), F32)
    outs = []
    for c in range(n // LANES):
        blk = m[:, c * LANES:(c + 1) * LANES]
        outs.append(jnp.dot(blk.astype(BF16), tri, preferred_element_type=F32) + carry)
        carry = carry + jnp.sum(blk, axis=1, keepdims=True)
    return jnp.concatenate(outs, axis=1)


SEL_SLOTS = 256
SEL_TOKENS = 1024


def _sel_kernel(lg_ref, idx_ref, gate_ref, dst_ref, seg_ref, aff_s, pos_s, dst_s, *, cap):
    lg = lg_ref[0]
    ne, n = lg.shape
    ex = jnp.exp(lg - jnp.max(lg, axis=0, keepdims=True))
    aff = ex / jnp.sum(ex, axis=0, keepdims=True)
    bits = lax.bitcast_convert_type(aff, I32)

    thr = jnp.zeros((ne, 1), I32)
    for bit in range(30, -1, -1):
        cand = thr | (1 << bit)
        cnt = jnp.sum((bits >= cand).astype(I32), axis=1, keepdims=True)
        thr = jnp.where(cnt >= cap, cand, thr)

    above = bits > thr
    tied = (bits == thr).astype(F32)
    need = (cap - jnp.sum(above.astype(I32), axis=1, keepdims=True)).astype(F32)
    sel = jnp.where(above, 1.0, jnp.where(_lane_cumsum_excl(tied) < need, tied, 0.0))

    per_token = jnp.sum(sel, axis=0, keepdims=True)
    seg_lo = _lane_cumsum_excl(jnp.broadcast_to(per_token, (ne, n)))[0:1, :]
    seg_ref[0, 0:1, :] = seg_lo
    seg_ref[0, 1:2, :] = seg_lo + per_token
    taken = seg_lo
    for e in range(ne):
        dst_s[e:e + 1, :] = taken
        taken = taken + sel[e:e + 1, :]

    aff_s[...] = aff
    pos_s[...] = (_lane_cumsum_excl(sel) + 1.0) * sel

    ss = min(SEL_SLOTS, cap)
    st = min(SEL_TOKENS, n)
    lane = lax.broadcasted_iota(I32, (ss, LANES), 1)
    s_base = lax.broadcasted_iota(I32, (ss, st), 0).astype(F32) + 1.0
    t_base = lax.broadcasted_iota(I32, (1, st), 1).astype(F32)
    idx_ref[...] = jnp.zeros(idx_ref.shape, idx_ref.dtype)
    gate_ref[...] = jnp.zeros(gate_ref.shape, gate_ref.dtype)
    dst_ref[...] = jnp.zeros(dst_ref.shape, dst_ref.dtype)

    def fold(v):
        out = v[:, :LANES]
        for q in range(1, st // LANES):
            out = out + v[:, q * LANES:(q + 1) * LANES]
        return out

    def per_expert(e, _):
        for sb in range(cap // ss):
            sv = s_base + float(sb * ss)

            def per_chunk(c, carry):
                ia, ga, da = carry
                t0 = pl.multiple_of(c * st, st)
                hit = pos_s[pl.ds(e, 1), pl.ds(t0, st)] == sv
                tok = t_base + lax.convert_element_type(c * st, F32)
                ia = ia + fold(jnp.where(hit, tok, 0.0))
                ga = ga + fold(jnp.where(hit, aff_s[pl.ds(e, 1), pl.ds(t0, st)], 0.0))
                da = da + fold(jnp.where(hit, dst_s[pl.ds(e, 1), pl.ds(t0, st)], 0.0))
                return ia, ga, da

            zero = jnp.zeros((ss, LANES), F32)
            ia, ga, da = lax.fori_loop(0, n // st, per_chunk, (zero, zero, zero))
            rows = slice(sb * ss, (sb + 1) * ss)
            mine = lane == e
            for ref, acc in ((idx_ref, ia), (gate_ref, ga), (dst_ref, da)):
                col = jnp.sum(acc, axis=1, keepdims=True).astype(ref.dtype)
                ref[0, rows, :] = jnp.where(mine, col, ref[0, rows, :])
        return 0

    lax.fori_loop(0, ne, per_expert, 0)


def _expert_choice(logits, cap):
    b, ne, n = logits.shape
    slot = pl.BlockSpec((1, cap, LANES), lambda i: (i, 0, 0))
    return pl.pallas_call(
        functools.partial(_sel_kernel, cap=cap),
        grid=(b,),
        in_specs=[pl.BlockSpec((1, ne, n), lambda i: (i, 0, 0))],
        out_specs=[slot, slot, slot, pl.BlockSpec((1, 2, n), lambda i: (i, 0, 0))],
        out_shape=[jax.ShapeDtypeStruct((b, cap, LANES), I32), jax.ShapeDtypeStruct((b, cap, LANES), F32),
                   jax.ShapeDtypeStruct((b, cap, LANES), I32), jax.ShapeDtypeStruct((b, 2, n), F32)],
        scratch_shapes=[pltpu.VMEM((ne, n), F32)] * 3,
        compiler_params=_params("parallel"),
        name="expert_choice",
    )(logits)


def _row_copy(src, dst, src_row, dst_row, sem):
    return pltpu.make_async_copy(src.at[pl.ds(src_row, 1), :], dst.at[pl.ds(dst_row, 1), :], sem)


def _ffn_kernel(src_ref, dst_ref, h_hbm, w1_ref, w3_ref, w2_ref, gate_ref, y_hbm, xin, xbf, acc, ybuf, gsem, psem, *,
                nf):
    tm, d = acc.shape
    half = d // 2
    per_step = tm // nf
    i = pl.program_id(0)
    f = pl.program_id(1)
    nt = pl.num_programs(0)

    def gather(tile, r):
        return _row_copy(h_hbm, xin, src_ref[tile * tm + r], r, gsem)

    def put(tile, r):
        return _row_copy(ybuf, y_hbm, r, dst_ref[tile * tm + r], psem)

    def all_gathered():
        pltpu.make_async_copy(h_hbm.at[pl.ds(0, tm), :], xin, gsem).wait()

    def all_put():
        pltpu.make_async_copy(ybuf, y_hbm.at[pl.ds(0, tm), :], psem).wait()

    def for_rows(fn):
        def body(r, carry):
            fn(r)
            return carry
        lax.fori_loop(0, tm, body, 0)

    @pl.when(jnp.logical_and(i == 0, f == 0))
    def _():
        for_rows(lambda r: gather(0, r).start())

    @pl.when(f == 0)
    def _():
        all_gathered()
        lo, hi = _unpack_bf16_pairs(xin[...])
        xbf[:, :half] = lo
        xbf[:, half:] = hi
        acc[...] = jnp.zeros(acc.shape, acc.dtype)

    nxt = jnp.where(i + 1 < nt, i + 1, 0)

    def step(with_puts):
        for r in range(per_step):
            gather(nxt, f * per_step + r).start()
        if with_puts:
            for r in range(per_step):
                put(i - 1, f * per_step + r).start()
        x = xbf[...]
        h1 = jnp.dot(x, w1_ref[0].astype(BF16), preferred_element_type=F32)
        h3 = jnp.dot(x, w3_ref[0].astype(BF16), preferred_element_type=F32)
        hid = (h1 * _sigmoid(h1) * h3).astype(BF16)
        acc[...] += jnp.dot(hid, w2_ref[0].astype(BF16), preferred_element_type=F32)

    lax.cond(i > 0, lambda: step(True), lambda: step(False))

    @pl.when(f == nf - 1)
    def _():
        @pl.when(i > 0)
        def _():
            all_put()

        ybuf[...] = _pack_bf16_pairs(acc[...] * gate_ref[...])

        @pl.when(i == nt - 1)
        def _():
            for_rows(lambda r: put(i, r).start())
            all_put()
            all_gathered()


def _expert_ffn(h2, w1, w3, w2, src_rows, dst_rows, gate, tm=512, tf=256):
    ne, d, ff = w1.shape
    total = src_rows.shape[0]
    s = total // ne
    tm = min(tm, s)
    tf = min(tf, ff)
    nf = ff // tf
    per_e = s // tm
    return pl.pallas_call(
        functools.partial(_ffn_kernel, nf=nf),
        grid_spec=pltpu.PrefetchScalarGridSpec(
            num_scalar_prefetch=2,
            grid=(ne * per_e, nf),
            in_specs=[
                pl.BlockSpec(memory_space=pl.ANY),
                pl.BlockSpec((1, d, tf), lambda i, f, sr, dr: (i // per_e, 0, f)),
                pl.BlockSpec((1, d, tf), lambda i, f, sr, dr: (i // per_e, 0, f)),
                pl.BlockSpec((1, tf, d), lambda i, f, sr, dr: (i // per_e, f, 0)),
                pl.BlockSpec((tm, 1), lambda i, f, sr, dr: (i, 0)),
            ],
            out_specs=pl.BlockSpec(memory_space=pl.ANY),
            scratch_shapes=[pltpu.VMEM((tm, d // 2), U32), pltpu.VMEM((tm, d), BF16), pltpu.VMEM((tm, d), F32),
                            pltpu.VMEM((tm, d // 2), U32), pltpu.SemaphoreType.DMA, pltpu.SemaphoreType.DMA],
        ),
        out_shape=jax.ShapeDtypeStruct((total, d // 2), U32),
        compiler_params=_params("arbitrary", "arbitrary"),
        name="expert_ffn",
    )(src_rows, dst_rows, h2, w1, w3, w2, gate)


SEG_ROWS = 256


def _combine_kernel(first_ref, last_ref, x_ref, lo_ref, hi_ref, gt_ref, g_ref, y_hbm, o_ref, buf, acc, sem, *,
                    rows_per_sample):
    b = pl.program_id(0)
    tile = b * pl.num_programs(1) + pl.program_id(1)
    first = first_ref[tile]
    last = last_ref[tile]
    r0 = (first // SUBLANES) * SUBLANES
    nch = jnp.where(last > first, (last - r0 + SEG_ROWS - 1) // SEG_ROWS, 0)

    def chunk_start(c):
        return pl.multiple_of(jnp.minimum(r0 + c * SEG_ROWS, rows_per_sample - SEG_ROWS), SUBLANES)

    def chunk_copy(c, slot):
        src = pl.multiple_of(b * rows_per_sample + chunk_start(c), SUBLANES)
        return pltpu.make_async_copy(y_hbm.at[pl.ds(src, SEG_ROWS), :], buf.at[slot], sem.at[slot])

    acc[...] = jnp.zeros(acc.shape, acc.dtype)

    @pl.when(nch > 0)
    def _():
        chunk_copy(0, 0).start()

    lo = lo_ref[0]
    hi = hi_ref[0]

    def body(c, carry):
        slot = c % 2

        @pl.when(c + 1 < nch)
        def _():
            chunk_copy(c + 1, 1 - slot).start()

        chunk_copy(c, slot).wait()
        row = (lax.broadcasted_iota(I32, (1, SEG_ROWS), 1) + chunk_start(c)).astype(F32)
        nominal = (r0 + c * SEG_ROWS).astype(F32)
        pick = jnp.logical_and(jnp.logical_and(lo <= row, row < hi), row >= nominal)
        pick = jnp.where(pick, 1.0, 0.0).astype(BF16)
        y_left, y_right = _unpack_bf16_pairs(buf[slot])
        half = y_left.shape[1]
        acc[:, :half] += jnp.dot(pick, y_left, preferred_element_type=F32)
        acc[:, half:] += jnp.dot(pick, y_right, preferred_element_type=F32)
        return carry

    lax.fori_loop(0, nch, body, 0)
    o_ref[0] = x_ref[0] + gt_ref[0] * (_rms(acc[...]) * g_ref[...])


def _combine(x1, y, seg, gate, g, tt=256):
    b, n, d = x1.shape
    rows_per_sample = y.shape[0] // b
    tt = min(tt, n)
    nj = n // tt
    seg_i = seg.astype(I32)
    first = seg_i[:, 0, ::tt].reshape(-1)
    last = seg_i[:, 1, tt - 1::tt].reshape(-1)
    rows = pl.BlockSpec((1, tt, d), lambda i, j, fr, lr: (i, j, 0))
    col = pl.BlockSpec((1, tt, 1), lambda i, j, fr, lr: (i, j, 0))
    return pl.pallas_call(
        functools.partial(_combine_kernel, rows_per_sample=rows_per_sample),
        grid_spec=pltpu.PrefetchScalarGridSpec(
            num_scalar_prefetch=2,
            grid=(b, nj),
            in_specs=[rows, col, col,
                      pl.BlockSpec((1, 1, d), lambda i, j, fr, lr: (i, 0, 0)),
                      pl.BlockSpec((1, d), lambda i, j, fr, lr: (0, 0)),
                      pl.BlockSpec(memory_space=pl.ANY)],
            out_specs=rows,
            scratch_shapes=[pltpu.VMEM((2, SEG_ROWS, d // 2), U32), pltpu.VMEM((tt, d), F32),
                            pltpu.SemaphoreType.DMA((2,))],
        ),
        out_shape=jax.ShapeDtypeStruct((b, n, d), F32),
        compiler_params=_params("arbitrary", "arbitrary"),
        name="combine_residual",
    )(first, last, x1, seg[:, 0, :, None], seg[:, 1, :, None], gate.reshape(b, 1, d), g.reshape(1, d), y)


def _layer(x, c, ctx, c_ctx, p):
    b, n, d = x.shape
    lc = ctx.shape[1]
    rw = RET_HEADS * RET_HD
    lw = p["conv_w"].shape[1]
    q_off, k_off, v_off, g_off = 0, rw, 2 * rw, 3 * rw
    lx_off = 4 * rw
    ly_off = lx_off + lw
    gr_off = ly_off + lw

    cvecs = jnp.zeros((SUBLANES, d), F32).at[:b].set(c).at[b].set(c_ctx)
    mod = _ada_mod(cvecs, p["ada_w"], p["ada_b"])
    sh1, sc1, gt1, sh2, sc2, gt2 = [mod[:b, i * d:(i + 1) * d] for i in range(6)]
    csh1 = jnp.broadcast_to(mod[b:b + 1, :d], (b, d))
    csc1 = jnp.broadcast_to(mod[b:b + 1, d:2 * d], (b, d))

    w_in = p["w_in"]
    h = _prenorm(x, p["n_pre_mix"], sc1, sh1).reshape(b * n, d)
    hc = _prenorm(ctx, p["n_pre_mix"], csc1, csh1).reshape(b * lc, d)

    k_scale = RET_HD ** -0.5
    q_tabs, k_tabs = _rope_tables(n, k_scale)
    q = _proj(h, w_in, q_off, rw, "rope", tables=q_tabs, rows_per_seq=n)
    k = _proj(h, w_in, k_off, rw, "rope", tables=k_tabs, rows_per_seq=n)
    v = _proj(h, w_in, v_off, rw)
    g = _proj(h, w_in, g_off, rw, "silu")
    lx = _proj(h, w_in, lx_off, lw, out_dtype=F32)
    ly = _proj(h, w_in, ly_off, lw, "gelu")
    gates = _proj(h, w_in, gr_off, 2 * d, "sigmoid")
    kc = _proj(hc, w_in, k_off, rw, "scale", scale=k_scale)
    vc = _proj(hc, w_in, v_off, rw)
    lxc = _proj(hc, w_in, lx_off, lw, out_dtype=F32)

    of, ob = _retention(q.reshape(b, n, rw), k.reshape(b, n, rw), v.reshape(b, n, rw),
                        kc.reshape(b, lc, rw), vc.reshape(b, lc, rw), p["ret_lg"])

    wg = jnp.concatenate([p["gate_a_w"], p["gate_x_w"]], axis=-1).astype(BF16)
    hd = lw // LRU_HEADS
    bg = jnp.concatenate([p["gate_a_b"].reshape(2, LRU_HEADS, hd), p["gate_x_b"].reshape(2, LRU_HEADS, hd)], axis=-1)
    lru_w = (p["conv_w"], p["conv_b"], wg, bg, p["lam"])
    zero_state = jnp.zeros((b, lw), F32)
    hfc, hbc = _lru(lxc.reshape(b, lc, lw), *lru_w, zero_state, zero_state)
    hf, hb = _lru(lx.reshape(b, n, lw), *lru_w, hfc[:, -1], hbc[:, 0])

    a = _mergeprep(of.reshape(b * n, rw), ob.reshape(b * n, rw), g,
                   hf.reshape(b * n, lw), hb.reshape(b * n, lw), ly)
    merged = _merge(a, p["w_ret_out"], p["w_lru_out"], gates)
    mix = _proj(merged, p["w_out"], 0, d, out_dtype=F32)
    x1, h2, logits = _post_mix(x, mix.reshape(b, n, d), gt1, p["n_post_mix"], p["n_pre_ffn"], sc2, sh2,
                               p["router_w"].T.astype(BF16))

    ne = p["router_w"].shape[1]
    cap = EC_CAPACITY * n // ne
    idx, gate, dst, seg = _expert_choice(logits, cap)
    by_expert = lambda a: jnp.transpose(a[:, :, :ne], (2, 0, 1))
    sample = jnp.arange(b, dtype=I32)[None, :, None]
    src_rows = (by_expert(idx) + sample * n).reshape(-1)
    dst_rows = (by_expert(dst) + sample * (ne * cap)).reshape(-1)
    y = _expert_ffn(h2.reshape(b * n, d // 2), p["w1"], p["w3"], p["w2"],
                    src_rows, dst_rows, by_expert(gate).reshape(-1, 1))
    return _combine(x1, y, seg, gt2, p["n_post_ffn"])


def kernel(x, c, ctx, c_ctx, ada_w, ada_b, norm_pre_mix, norm_post_mix, norm_pre_ffn, norm_post_ffn, w_in, ret_log_gamma, w_ret_out, lru_conv_w, lru_conv_b, lru_gate_a_w, lru_gate_a_b, lru_gate_x_w, lru_gate_x_b, lru_lambda, w_lru_out, w_out, router_w, expert_w1, expert_w3, expert_w2):
    depth = ada_w.shape[0]
    assert depth == 1, "context-stream update between layers is not implemented"
    l = 0
    p = {
        "ada_w": ada_w[l], "ada_b": ada_b[l],
        "n_pre_mix": norm_pre_mix[l], "n_post_mix": norm_post_mix[l],
        "n_pre_ffn": norm_pre_ffn[l], "n_post_ffn": norm_post_ffn[l],
        "w_in": w_in[l], "ret_lg": ret_log_gamma[l], "w_ret_out": w_ret_out[l],
        "conv_w": lru_conv_w[l], "conv_b": lru_conv_b[l],
        "gate_a_w": lru_gate_a_w[l], "gate_a_b": lru_gate_a_b[l],
        "gate_x_w": lru_gate_x_w[l], "gate_x_b": lru_gate_x_b[l], "lam": lru_lambda[l],
        "w_lru_out": w_lru_out[l], "w_out": w_out[l], "router_w": router_w[l],
        "w1": expert_w1[l], "w3": expert_w3[l], "w2": expert_w2[l],
    }
    return _layer(x, c, ctx, c_ctx, p)
```

```python
import functools

import jax
import jax.numpy as jnp
from jax import lax
from jax.experimental import pallas as pl
from jax.experimental.pallas import tpu as pltpu

F32 = jnp.float32
BF16 = jnp.bfloat16
I32 = jnp.int32

EPS = 1e-6
GRID_W = 64
RET_HEADS = 8
RET_HD = 256
ROPE_BASE = 10000.0
LRU_HEADS = 8
LRU_C = 8.0
CONV_W = 4
CONV_LEFT = 2
N_EXPERTS = 16
EC_CAPACITY = 2

LANES = 128
SUBLANES = 8
VMEM_LIMIT_BYTES = 60 * 1024 * 1024


def _params(*sem):
    return pltpu.CompilerParams(dimension_semantics=sem, vmem_limit_bytes=VMEM_LIMIT_BYTES)


def _rms(x):
    return x * lax.rsqrt(jnp.mean(x * x, axis=-1, keepdims=True) + EPS)


def _sigmoid(x):
    return 0.5 * jnp.tanh(0.5 * x) + 0.5


def _gelu_tanh(x):
    return 0.5 * x * (1.0 + jnp.tanh(0.7978845608028654 * (x + 0.044715 * (x * x * x))))


def _ada_kernel(c_ref, w_ref, b_ref, o_ref):
    c = c_ref[...]
    s = (c * _sigmoid(c)).astype(BF16)
    o_ref[...] = jnp.dot(s, w_ref[...].astype(BF16), preferred_element_type=F32) + b_ref[...]


def _ada_mod(cvecs, ada_w, ada_b, tn=512):
    d, w = ada_w.shape
    tn = min(tn, w)
    return pl.pallas_call(
        _ada_kernel,
        grid=(w // tn,),
        in_specs=[
            pl.BlockSpec((SUBLANES, d), lambda j: (0, 0)),
            pl.BlockSpec((d, tn), lambda j: (0, j)),
            pl.BlockSpec((1, tn), lambda j: (0, j)),
        ],
        out_specs=pl.BlockSpec((SUBLANES, tn), lambda j: (0, j)),
        out_shape=jax.ShapeDtypeStruct((SUBLANES, w), F32),
        compiler_params=_params("parallel"),
        name="ada_mod",
    )(cvecs, ada_w, ada_b.reshape(1, w))


def _prenorm_kernel(x_ref, g_ref, sc_ref, sh_ref, o_ref):
    y = _rms(x_ref[0]) * g_ref[...]
    o_ref[0] = (y * (1.0 + sc_ref[0]) + sh_ref[0]).astype(o_ref.dtype)


def _prenorm(x, g, scale, shift, tr=256):
    b, n, d = x.shape
    tr = min(tr, n)
    return pl.pallas_call(
        _prenorm_kernel,
        grid=(b, n // tr),
        in_specs=[
            pl.BlockSpec((1, tr, d), lambda i, j: (i, j, 0)),
            pl.BlockSpec((1, d), lambda i, j: (0, 0)),
            pl.BlockSpec((1, 1, d), lambda i, j: (i, 0, 0)),
            pl.BlockSpec((1, 1, d), lambda i, j: (i, 0, 0)),
        ],
        out_specs=pl.BlockSpec((1, tr, d), lambda i, j: (i, j, 0)),
        out_shape=jax.ShapeDtypeStruct((b, n, d), BF16),
        compiler_params=_params("parallel", "parallel"),
        name="prenorm",
    )(x, g.reshape(1, d), scale.reshape(b, 1, d), shift.reshape(b, 1, d))


def _proj_kernel(*refs, epilogue, scale):
    if epilogue == "rope":
        h_ref, w_ref, cos_ref, sin_ref, o_ref, wbf = refs
    else:
        h_ref, w_ref, o_ref, wbf = refs

    @pl.when(pl.program_id(1) == 0)
    def _():
        wbf[...] = w_ref[...].astype(BF16)

    acc = jnp.dot(h_ref[...], wbf[...], preferred_element_type=F32)
    if epilogue == "rope":
        for s in range(acc.shape[1] // LANES):
            a = acc[:, s * LANES:(s + 1) * LANES]
            ts = (s * LANES) % RET_HD
            cos = cos_ref[:, ts:ts + LANES]
            sin = sin_ref[:, ts:ts + LANES]
            o_ref[:, s * LANES:(s + 1) * LANES] = (
                a * cos + pltpu.roll(a, LANES // 2, 1) * sin).astype(o_ref.dtype)
        return
    if epilogue == "silu":
        acc = acc * _sigmoid(acc)
    elif epilogue == "gelu":
        acc = _gelu_tanh(acc)
    elif epilogue == "sigmoid":
        acc = _sigmoid(acc)
    elif epilogue == "scale":
        acc = acc * scale
    o_ref[...] = acc.astype(o_ref.dtype)


def _proj(h, w, col_off, width, epilogue="none", out_dtype=BF16, tables=None, rows_per_seq=None,
          scale=1.0, tm=1024, tn=512):
    m, k = h.shape
    tm = min(tm, m)
    tn = min(tn, width)
    if rows_per_seq is not None:
        tm = min(tm, rows_per_seq)
    jo = col_off // tn
    in_specs = [
        pl.BlockSpec((tm, k), lambda j, i: (i, 0)),
        pl.BlockSpec((k, tn), lambda j, i: (0, jo + j)),
    ]
    args = [h, w]
    if epilogue == "rope":
        nblk = rows_per_seq // tm
        tspec = pl.BlockSpec((tm, RET_HD), lambda j, i: (i % nblk, 0))
        in_specs += [tspec, tspec]
        args += list(tables)
    return pl.pallas_call(
        functools.partial(_proj_kernel, epilogue=epilogue, scale=scale),
        grid=(width // tn, m // tm),
        in_specs=in_specs,
        out_specs=pl.BlockSpec((tm, tn), lambda j, i: (i, j)),
        out_shape=jax.ShapeDtypeStruct((m, width), out_dtype),
        scratch_shapes=[pltpu.VMEM((k, tn), BF16)],
        compiler_params=_params("parallel", "arbitrary"),
        name="proj_" + epilogue,
    )(*args)


def _rope_tables(n, k_scale):
    quarter = RET_HD // 4
    freqs = ROPE_BASE ** (-jnp.arange(quarter, dtype=F32) / quarter)
    rows = n // GRID_W
    row = jnp.repeat(jnp.arange(rows, dtype=F32), GRID_W)
    col = jnp.tile(jnp.arange(GRID_W, dtype=F32), rows)

    def part(pos):
        ang = pos[:, None] * freqs[None, :]
        c, s = jnp.cos(ang), jnp.sin(ang)
        return jnp.concatenate([c, c], axis=-1), jnp.concatenate([-s, s], axis=-1)

    cr, sr = part(row)
    cc, sc = part(col)
    cos = jnp.concatenate([cr, cc], axis=-1)
    sin = jnp.concatenate([sr, sc], axis=-1)
    return (cos, sin), (cos * k_scale, sin * k_scale)


def _dot_nt(a, b):
    return lax.dot_general(a, b, (((1,), (1,)), ((), ())), preferred_element_type=F32)


def _dot_tn(a, b):
    return lax.dot_general(a, b, (((0,), (0,)), ((), ())), preferred_element_type=F32)


def _ret_kernel(lg_ref, qf_ref, kf_ref, vf_ref, qb_ref, kb_ref, vb_ref, kc_ref, vc_ref,
                of_ref, ob_ref, sf_ref, sb_ref, *, csz):
    h = pl.program_id(1)
    c = pl.program_id(2)
    lgf = lg_ref[0, h]
    lgb = lg_ref[1, h]
    nsub = qf_ref.shape[1] // csz

    @pl.when(c == 0)
    def _():
        kc = kc_ref[0].astype(F32)
        vc = vc_ref[0]
        lc = kc.shape[0]
        m = lax.broadcasted_iota(I32, (lc, 1), 0).astype(F32)
        sf_ref[...] = _dot_tn((kc * jnp.exp((lc - 1.0 - m) * lgf)).astype(BF16), vc)
        sb_ref[...] = _dot_tn((kc * jnp.exp(m * lgb)).astype(BF16), vc)

    qi = lax.broadcasted_iota(I32, (csz, csz), 0)
    ki = lax.broadcasted_iota(I32, (csz, csz), 1)
    j = lax.broadcasted_iota(I32, (csz, 1), 0).astype(F32)

    def direction(q_ref, k_ref, v_ref, o_ref, s_ref, lg, dist, q_pow, k_pow, order):
        intra = jnp.where(dist >= 0, jnp.exp(jnp.maximum(dist, 0).astype(F32) * lg), 0.0)
        q_dec = jnp.exp(q_pow * lg)
        k_dec = jnp.exp(k_pow * lg)
        chunk_dec = jnp.exp(jnp.full((1, 1), csz, F32) * lg)
        state = s_ref[...]
        for sub in order:
            rows = slice(sub * csz, (sub + 1) * csz)
            q = q_ref[0, rows, :]
            k = k_ref[0, rows, :]
            v = v_ref[0, rows, :]
            s = _dot_nt(q, k) * intra
            o = jnp.dot(s.astype(BF16), v, preferred_element_type=F32)
            o_ref[0, rows, :] = o + jnp.dot(q, state.astype(BF16), preferred_element_type=F32) * q_dec
            kd = (k.astype(F32) * k_dec).astype(BF16)
            state = state * chunk_dec + _dot_tn(kd, v)
        s_ref[...] = state

    direction(qf_ref, kf_ref, vf_ref, of_ref, sf_ref, lgf, qi - ki, j + 1.0, csz - 1.0 - j, range(nsub))
    direction(qb_ref, kb_ref, vb_ref, ob_ref, sb_ref, lgb, ki - qi, csz - j, j, range(nsub - 1, -1, -1))


def _retention(q, k, v, kc, vc, log_gamma, chunk=256, block=1024):
    b, n, hw = q.shape
    dh = RET_HD
    nh = hw // dh
    lc = kc.shape[1]
    chunk = min(chunk, n)
    block = min(block, n)
    nc = n // block
    fwd = pl.BlockSpec((1, block, dh), lambda bi, hi, ci: (bi, ci, hi))
    bwd = pl.BlockSpec((1, block, dh), lambda bi, hi, ci: (bi, nc - 1 - ci, hi))
    ctx = pl.BlockSpec((1, lc, dh), lambda bi, hi, ci: (bi, 0, hi))
    return pl.pallas_call(
        functools.partial(_ret_kernel, csz=chunk),
        grid=(b, nh, nc),
        in_specs=[pl.BlockSpec(memory_space=pltpu.SMEM), fwd, fwd, fwd, bwd, bwd, bwd, ctx, ctx],
        out_specs=[fwd, bwd],
        out_shape=[jax.ShapeDtypeStruct((b, n, hw), F32)] * 2,
        scratch_shapes=[pltpu.VMEM((dh, dh), F32), pltpu.VMEM((dh, dh), F32)],
        compiler_params=_params("parallel", "parallel", "arbitrary"),
        name="retention",
    )(log_gamma.astype(F32), q, k, v, q, k, v, kc, vc)


HALO = SUBLANES
SCAN_UNROLL = 8


def _lru_kernel(mf_ref, pf_ref, nf_ref, mb_ref, pb_ref, nb_ref, cw_ref, cb_ref, wg_ref, bg_ref, lam_ref,
                h0f_ref, h0b_ref, hf_ref, hb_ref, ext_s, a_s, b_s, cf_s, cb_s):
    t = pl.program_id(2)
    nt = pl.num_programs(2)
    tt = mf_ref.shape[1]
    w = mf_ref.shape[2]

    @pl.when(t == 0)
    def _():
        cf_s[...] = h0f_ref[0]
        cb_s[...] = h0b_ref[0]

    def coeffs(m_ref, p_ref, n_ref, has_prev, has_next, d):
        ext_s[0:HALO, :] = jnp.where(has_prev, p_ref[0], 0.0)
        ext_s[HALO:HALO + tt, :] = m_ref[0]
        ext_s[HALO + tt:, :] = jnp.where(has_next, n_ref[0], 0.0)
        u = cb_ref[...]
        for i in range(CONV_W):
            off = HALO + i - CONV_LEFT
            u = u + ext_s[off:off + tt, :] * cw_ref[i:i + 1, :]
        g = jnp.dot(u.astype(BF16), wg_ref[d, 0], preferred_element_type=F32) + bg_ref[d:d + 1, :]
        r = _sigmoid(g[:, :w])
        ig = _sigmoid(g[:, w:])
        lam = lam_ref[d:d + 1, :]
        softplus = jnp.maximum(-lam, 0.0) + jnp.log(1.0 + jnp.exp(-jnp.abs(lam)))
        log_a = -LRU_C * r * softplus
        a = jnp.exp(log_a)
        a_s[...] = a
        b_s[...] = jnp.sqrt(-jnp.tanh(log_a) * (a * a + 1.0)) * (ig * u)

    row = lax.broadcasted_iota(I32, (SUBLANES, w), 0)
    ntile = tt // SUBLANES

    coeffs(mf_ref, pf_ref, nf_ref, t > 0, t < nt - 1, 0)

    def fwd_body(i, carry):
        r0 = pl.multiple_of(i * SUBLANES, SUBLANES)
        a = a_s[pl.ds(r0, SUBLANES), :]
        b = b_s[pl.ds(r0, SUBLANES), :]
        for s in (1, 2, 4):
            keep = row >= s
            b = jnp.where(keep, a * pltpu.roll(b, s, 0) + b, b)
            a = jnp.where(keep, a * pltpu.roll(a, s, 0), a)
        hcur = a * carry + b
        hf_ref[0, pl.ds(r0, SUBLANES), :] = hcur
        return hcur[SUBLANES - 1:SUBLANES, :]

    cf_s[...] = lax.fori_loop(0, ntile, fwd_body, cf_s[...], unroll=min(SCAN_UNROLL, ntile))

    coeffs(mb_ref, pb_ref, nb_ref, t < nt - 1, t > 0, 1)

    def bwd_body(i, carry):
        r0 = pl.multiple_of((ntile - 1 - i) * SUBLANES, SUBLANES)
        a = a_s[pl.ds(r0, SUBLANES), :]
        b = b_s[pl.ds(r0, SUBLANES), :]
        for s in (1, 2, 4):
            keep = row < SUBLANES - s
            b = jnp.where(keep, a * pltpu.roll(b, SUBLANES - s, 0) + b, b)
            a = jnp.where(keep, a * pltpu.roll(a, SUBLANES - s, 0), a)
        hcur = a * carry + b
        hb_ref[0, pl.ds(r0, SUBLANES), :] = hcur
        return hcur[0:1, :]

    cb_s[...] = lax.fori_loop(0, ntile, bwd_body, cb_s[...], unroll=min(SCAN_UNROLL, ntile))


def _lru(lx, conv_w, conv_b, wg, bg, lam, h0f, h0b, tt=512):
    b, n, wtot = lx.shape
    nh = wg.shape[1]
    hd = wtot // nh
    tt = min(tt, n)
    nt = n // tt
    hb_per = tt // HALO
    nhalo = n // HALO

    def main_spec(rev):
        return pl.BlockSpec((1, tt, hd), lambda bi, hi, ti: (bi, (nt - 1 - ti) if rev else ti, hi))

    def prev_spec(rev):
        return pl.BlockSpec(
            (1, HALO, hd),
            lambda bi, hi, ti: (bi, jnp.maximum(((nt - 1 - ti) if rev else ti) * hb_per - 1, 0), hi))

    def next_spec(rev):
        return pl.BlockSpec(
            (1, HALO, hd),
            lambda bi, hi, ti: (bi, jnp.minimum((((nt - 1 - ti) if rev else ti) + 1) * hb_per, nhalo - 1), hi))

    head_vec = lambda rows: pl.BlockSpec((rows, hd), lambda bi, hi, ti: (0, hi))
    state = pl.BlockSpec((1, 1, hd), lambda bi, hi, ti: (bi, 0, hi))
    bg2 = bg.reshape(2, nh * 2 * hd)
    return pl.pallas_call(
        _lru_kernel,
        grid=(b, nh, nt),
        in_specs=[
            main_spec(False), prev_spec(False), next_spec(False),
            main_spec(True), prev_spec(True), next_spec(True),
            head_vec(CONV_W), head_vec(1),
            pl.BlockSpec((2, 1, hd, 2 * hd), lambda bi, hi, ti: (0, hi, 0, 0)),
            pl.BlockSpec((2, 2 * hd), lambda bi, hi, ti: (0, hi)),
            head_vec(2), state, state,
        ],
        out_specs=[main_spec(False), main_spec(True)],
        out_shape=[jax.ShapeDtypeStruct((b, n, wtot), F32)] * 2,
        scratch_shapes=[pltpu.VMEM((tt + 2 * HALO, hd), F32), pltpu.VMEM((tt, hd), F32), pltpu.VMEM((tt, hd), F32),
                        pltpu.VMEM((1, hd), F32), pltpu.VMEM((1, hd), F32)],
        compiler_params=_params("parallel", "parallel", "arbitrary"),
        name="rglru",
    )(lx, lx, lx, lx, lx, lx, conv_w, conv_b.reshape(1, wtot), wg, bg2, lam,
      h0f.reshape(b, 1, wtot), h0b.reshape(b, 1, wtot))


def _mergeprep_kernel(of_ref, ob_ref, g_ref, hf_ref, hb_ref, ly_ref, o_ref):
    rw = of_ref.shape[1]
    for h in range(rw // RET_HD):
        sl = slice(h * RET_HD, (h + 1) * RET_HD)
        o = _rms(of_ref[:, sl] + ob_ref[:, sl])
        o_ref[:, sl] = (o * g_ref[:, sl].astype(F32)).astype(o_ref.dtype)
    o_ref[:, rw:] = (ly_ref[...].astype(F32) * (hf_ref[...] + hb_ref[...])).astype(o_ref.dtype)


def _mergeprep(of, ob, g, hf, hb, ly, tr=256):
    m, rw = of.shape
    lw = hf.shape[1]
    tr = min(tr, m)
    spec = lambda wd: pl.BlockSpec((tr, wd), lambda i: (i, 0))
    return pl.pallas_call(
        _mergeprep_kernel,
        grid=(m // tr,),
        in_specs=[spec(rw), spec(rw), spec(rw), spec(lw), spec(lw), spec(lw)],
        out_specs=spec(rw + lw),
        out_shape=jax.ShapeDtypeStruct((m, rw + lw), BF16),
        compiler_params=_params("parallel"),
        name="mergeprep",
    )(of, ob, g, hf, hb, ly)


def _merge_kernel(a_ref, wr_ref, wl_ref, gr_ref, gl_ref, o_ref, wr_bf, wl_bf):
    rw = wr_ref.shape[0]

    @pl.when(pl.program_id(1) == 0)
    def _():
        wr_bf[...] = wr_ref[...].astype(BF16)
        wl_bf[...] = wl_ref[...].astype(BF16)

    ret = jnp.dot(a_ref[:, :rw], wr_bf[...], preferred_element_type=F32)
    lru = jnp.dot(a_ref[:, rw:], wl_bf[...], preferred_element_type=F32)
    o_ref[...] = (gr_ref[...].astype(F32) * ret + gl_ref[...].astype(F32) * lru).astype(o_ref.dtype)


def _merge(a, w_ret_out, w_lru_out, gates, tm=1024, tn=512):
    m = a.shape[0]
    rw, d = w_ret_out.shape
    lw = w_lru_out.shape[0]
    tm = min(tm, m)
    tn = min(tn, d)
    nj = d // tn
    return pl.pallas_call(
        _merge_kernel,
        grid=(nj, m // tm),
        in_specs=[
            pl.BlockSpec((tm, rw + lw), lambda j, i: (i, 0)),
            pl.BlockSpec((rw, tn), lambda j, i: (0, j)),
            pl.BlockSpec((lw, tn), lambda j, i: (0, j)),
            pl.BlockSpec((tm, tn), lambda j, i: (i, j)),
            pl.BlockSpec((tm, tn), lambda j, i: (i, nj + j)),
        ],
        out_specs=pl.BlockSpec((tm, tn), lambda j, i: (i, j)),
        out_shape=jax.ShapeDtypeStruct((m, d), BF16),
        scratch_shapes=[pltpu.VMEM((rw, tn), BF16), pltpu.VMEM((lw, tn), BF16)],
        compiler_params=_params("parallel", "arbitrary"),
        name="merge",
    )(a, w_ret_out, w_lru_out, gates, gates)


def _post_mix_kernel(x_ref, mix_ref, gt_ref, g1_ref, g2_ref, sc_ref, sh_ref, rw_ref, x1_ref, h2_ref, lg_ref):
    x1 = x_ref[0] + gt_ref[0] * (_rms(mix_ref[0]) * g1_ref[...])
    x1_ref[0] = x1
    h2 = (_rms(x1) * g2_ref[...]) * (1.0 + sc_ref[0]) + sh_ref[0]
    h2_ref[0] = h2
    lg_ref[0] = _dot_nt(rw_ref[...], h2.astype(BF16))


def _post_mix(x, mix, gate, g_post, g_pre, scale, shift, router_w_t, tr=256):
    b, n, d = x.shape
    e = router_w_t.shape[0]
    tr = min(tr, n)
    rows = pl.BlockSpec((1, tr, d), lambda i, j: (i, j, 0))
    vec = pl.BlockSpec((1, d), lambda i, j: (0, 0))
    bvec = pl.BlockSpec((1, 1, d), lambda i, j: (i, 0, 0))
    return pl.pallas_call(
        _post_mix_kernel,
        grid=(b, n // tr),
        in_specs=[rows, rows, bvec, vec, vec, bvec, bvec, pl.BlockSpec((e, d), lambda i, j: (0, 0))],
        out_specs=[rows, rows, pl.BlockSpec((1, e, tr), lambda i, j: (i, 0, j))],
        out_shape=[jax.ShapeDtypeStruct((b, n, d), F32), jax.ShapeDtypeStruct((b, n, d), F32),
                   jax.ShapeDtypeStruct((b, e, n), F32)],
        compiler_params=_params("parallel", "parallel"),
        name="post_mix",
    )(x, mix, gate.reshape(b, 1, d), g_post.reshape(1, d), g_pre.reshape(1, d),
      scale.reshape(b, 1, d), shift.reshape(b, 1, d), router_w_t)


def _lane_cumsum_excl(m):
    e, n = m.shape
    tri = (lax.broadcasted_iota(I32, (LANES, LANES), 0) < lax.broadcasted_iota(I32, (LANES, LANES), 1)).astype(BF16)
    carry = jnp.zeros((e, 1), F32)
    outs = []
    for c in range(n // LANES):
        blk = m[:, c * LANES:(c + 1) * LANES]
        outs.append(jnp.dot(blk.astype(BF16), tri, preferred_element_type=F32) + carry)
        carry = carry + jnp.sum(blk, axis=1, keepdims=True)
    return jnp.concatenate(outs, axis=1)


SEL_SLOTS = 256
SEL_TOKENS = 512


def _sel_kernel(lg_ref, idx_ref, gate_ref, dst_ref, seg_ref, aff_s, pos_s, dst_s, starts_v, starts_sm, sem, *, cap):
    lg = lg_ref[0]
    ne, n = lg.shape
    ex = jnp.exp(lg - jnp.max(lg, axis=0, keepdims=True))
    aff = ex / jnp.sum(ex, axis=0, keepdims=True)
    thr_bits = jnp.zeros((ne, 1), I32)
    for bit in range(30, -1, -1):
        cand = thr_bits | (1 << bit)
        cnt = jnp.sum((aff >= lax.bitcast_convert_type(cand, F32)).astype(I32), axis=1, keepdims=True)
        thr_bits = jnp.where(cnt >= cap, cand, thr_bits)
    thr = lax.bitcast_convert_type(thr_bits, F32)

    above = aff > thr
    tied = (aff == thr).astype(F32)
    need = (cap - jnp.sum(above.astype(I32), axis=1, keepdims=True)).astype(F32)
    sel = jnp.where(above, 1.0, jnp.where(_lane_cumsum_excl(tied) < need, tied, 0.0))

    per_token = jnp.sum(sel, axis=0, keepdims=True)
    seg_lo = _lane_cumsum_excl(jnp.broadcast_to(per_token, (ne, n)))[0:1, :]
    seg_ref[0, 0:1, :] = seg_lo
    seg_ref[0, 1:2, :] = seg_lo + per_token
    taken = seg_lo
    for e in range(ne):
        dst_s[e:e + 1, :] = taken
        taken = taken + sel[e:e + 1, :]

    aff_s[...] = aff
    before = _lane_cumsum_excl(sel)
    pos_s[...] = (before + 1.0) * sel

    ss = min(SEL_SLOTS, cap)
    st = min(SEL_TOKENS, n)
    nchunk = n // st

    lane_e = lax.broadcasted_iota(I32, (ne, LANES), 1)
    starts = jnp.full((ne, LANES), cap, I32)
    for c in range(nchunk):
        starts = jnp.where(lane_e == c, before[:, c * st:c * st + 1].astype(I32), starts)
    starts_v[...] = starts
    to_smem = pltpu.make_async_copy(starts_v, starts_sm, sem)
    to_smem.start()
    to_smem.wait()

    lane = lax.broadcasted_iota(I32, (ss, LANES), 1)
    s_base = lax.broadcasted_iota(I32, (ss, st), 0).astype(F32) + 1.0
    t_base = lax.broadcasted_iota(I32, (1, st), 1).astype(F32)
    idx_ref[...] = jnp.zeros(idx_ref.shape, idx_ref.dtype)
    gate_ref[...] = jnp.zeros(gate_ref.shape, gate_ref.dtype)
    dst_ref[...] = jnp.zeros(dst_ref.shape, dst_ref.dtype)

    def fold(v):
        out = v[:, :LANES]
        for q in range(1, st // LANES):
            out = out + v[:, q * LANES:(q + 1) * LANES]
        return out

    def per_expert(e, _):
        for sb in range(cap // ss):
            sv = s_base + float(sb * ss)

            def per_chunk(c, carry):
                ia, ga, da = carry
                t0 = pl.multiple_of(c * st, st)
                hit = pos_s[pl.ds(e, 1), pl.ds(t0, st)] == sv
                tok = t_base + lax.convert_element_type(c * st, F32)
                ia = ia + fold(jnp.where(hit, tok, 0.0))
                ga = ga + fold(jnp.where(hit, aff_s[pl.ds(e, 1), pl.ds(t0, st)], 0.0))
                da = da + fold(jnp.where(hit, dst_s[pl.ds(e, 1), pl.ds(t0, st)], 0.0))
                return ia, ga, da

            c_lo = jnp.int32(0)
            c_hi = jnp.int32(0)
            for c in range(nchunk):
                c_lo = c_lo + (starts_sm[e, c + 1] <= sb * ss).astype(I32)
                c_hi = c_hi + (starts_sm[e, c] < (sb + 1) * ss).astype(I32)
            zero = jnp.zeros((ss, LANES), F32)
            ia, ga, da = lax.fori_loop(c_lo, c_hi, per_chunk, (zero, zero, zero))
            rows = slice(sb * ss, (sb + 1) * ss)
            mine = lane == e
            for ref, acc in ((idx_ref, ia), (gate_ref, ga), (dst_ref, da)):
                col = jnp.sum(acc, axis=1, keepdims=True).astype(ref.dtype)
                ref[0, rows, :] = jnp.where(mine, col, ref[0, rows, :])
        return 0

    lax.fori_loop(0, ne, per_expert, 0)


def _expert_choice(logits, cap):
    b, ne, n = logits.shape
    slot = pl.BlockSpec((1, cap, LANES), lambda i: (i, 0, 0))
    return pl.pallas_call(
        functools.partial(_sel_kernel, cap=cap),
        grid=(b,),
        in_specs=[pl.BlockSpec((1, ne, n), lambda i: (i, 0, 0))],
        out_specs=[slot, slot, slot, pl.BlockSpec((1, 2, n), lambda i: (i, 0, 0))],
        out_shape=[jax.ShapeDtypeStruct((b, cap, LANES), I32), jax.ShapeDtypeStruct((b, cap, LANES), F32),
                   jax.ShapeDtypeStruct((b, cap, LANES), I32), jax.ShapeDtypeStruct((b, 2, n), F32)],
        scratch_shapes=[pltpu.VMEM((ne, n), F32)] * 3 + [pltpu.VMEM((ne, LANES), I32), pltpu.SMEM((ne, LANES), I32),
                                                       pltpu.SemaphoreType.DMA],
        compiler_params=_params("arbitrary"),
        name="expert_choice",
    )(logits)


def _row_copy(src, dst, src_row, dst_row, sem):
    return pltpu.make_async_copy(src.at[pl.ds(src_row, 1), :], dst.at[pl.ds(dst_row, 1), :], sem)


def _ffn_kernel(src_ref, dst_ref, h_hbm, w1_ref, w3_ref, w2l_ref, w2r_ref, gate_ref, y_hbm,
                xin, xbf, hid, ybuf, gsem, psem, *, na, nb):
    tm, d = xbf.shape
    half = d // 2
    tf = w1_ref.shape[2]
    tn = w2l_ref.shape[2]
    gathers_per_step = tm // (na + nb)
    puts_per_step = tm // na
    i = pl.program_id(0)
    f = pl.program_id(1)
    nt = pl.num_programs(0)

    def gather(tile, r):
        return _row_copy(h_hbm, xin, src_ref[tile * tm + r], r, gsem)

    def put(tile, r):
        return _row_copy(ybuf, y_hbm, r, dst_ref[tile * tm + r], psem)

    def all_gathered():
        pltpu.make_async_copy(h_hbm.at[pl.ds(0, tm), :], xin, gsem).wait()

    def all_put():
        pltpu.make_async_copy(ybuf, y_hbm.at[pl.ds(0, tm), :], psem).wait()

    def for_rows(fn):
        def body(r, carry):
            fn(r)
            return carry
        lax.fori_loop(0, tm, body, 0)

    @pl.when(jnp.logical_and(i == 0, f == 0))
    def _():
        for_rows(lambda r: gather(0, r).start())

    @pl.when(f == 0)
    def _():
        all_gathered()
        xbf[...] = xin[...].astype(BF16)

    nxt = jnp.where(i + 1 < nt, i + 1, 0)

    def prefetch():
        for r in range(gathers_per_step):
            gather(nxt, f * gathers_per_step + r).start()

    def up(with_puts):
        prefetch()
        if with_puts:
            for r in range(puts_per_step):
                put(i - 1, f * puts_per_step + r).start()
        x = xbf[...]
        h1 = jnp.dot(x, w1_ref[0].astype(BF16), preferred_element_type=F32)
        h3 = jnp.dot(x, w3_ref[0].astype(BF16), preferred_element_type=F32)
        hid[:, pl.ds(pl.multiple_of(f * tf, tf), tf)] = (h1 * _sigmoid(h1) * h3).astype(BF16)

    def down():
        prefetch()

        @pl.when(jnp.logical_and(f == na, i > 0))
        def _():
            all_put()

        h = hid[...]
        gate = gate_ref[...]
        left = jnp.dot(h, w2l_ref[0].astype(BF16), preferred_element_type=F32) * gate
        right = jnp.dot(h, w2r_ref[0].astype(BF16), preferred_element_type=F32) * gate
        col = pl.multiple_of((f - na) * tn, tn)
        ybuf[:, pl.ds(col, tn)] = left
        ybuf[:, pl.ds(pl.multiple_of(half + col, tn), tn)] = right

    lax.cond(f < na, lambda: lax.cond(i > 0, lambda: up(True), lambda: up(False)), down)

    @pl.when(jnp.logical_and(f == na + nb - 1, i == nt - 1))
    def _():
        for_rows(lambda r: put(i, r).start())
        all_put()
        all_gathered()


def _expert_ffn(h2, w1, w3, w2, src_rows, dst_rows, gate, tm=512, tf=256):
    ne, d, ff = w1.shape
    total = src_rows.shape[0]
    s = total // ne
    tm = min(tm, s)
    tf = min(tf, ff)
    tn = min(tf, d // 2)
    na = ff // tf
    nb = (d // 2) // tn
    per_e = s // tm
    assert tm % (na + nb) == 0 and tm % na == 0, "row copies are issued in equal shares per grid step"
    up_tile = lambda i, f, sr, dr: (i // per_e, 0, jnp.minimum(f, na - 1))
    return pl.pallas_call(
        functools.partial(_ffn_kernel, na=na, nb=nb),
        grid_spec=pltpu.PrefetchScalarGridSpec(
            num_scalar_prefetch=2,
            grid=(ne * per_e, na + nb),
            in_specs=[
                pl.BlockSpec(memory_space=pl.ANY),
                pl.BlockSpec((1, d, tf), up_tile),
                pl.BlockSpec((1, d, tf), up_tile),
                pl.BlockSpec((1, ff, tn), lambda i, f, sr, dr: (i // per_e, 0, jnp.maximum(f - na, 0))),
                pl.BlockSpec((1, ff, tn), lambda i, f, sr, dr: (i // per_e, 0, nb + jnp.maximum(f - na, 0))),
                pl.BlockSpec((tm, 1), lambda i, f, sr, dr: (i, 0)),
            ],
            out_specs=pl.BlockSpec(memory_space=pl.ANY),
            scratch_shapes=[pltpu.VMEM((tm, d), F32), pltpu.VMEM((tm, d), BF16), pltpu.VMEM((tm, ff), BF16),
                            pltpu.VMEM((tm, d), F32), pltpu.SemaphoreType.DMA, pltpu.SemaphoreType.DMA],
        ),
        out_shape=jax.ShapeDtypeStruct((total, d), F32),
        compiler_params=_params("arbitrary", "arbitrary"),
        name="expert_ffn",
    )(src_rows, dst_rows, h2, w1, w3, w2, w2, gate)


SEG_ROWS = 256


def _combine_kernel(first_ref, last_ref, x_ref, lo_ref, hi_ref, gt_ref, g_ref, y_hbm, o_ref, buf, acc, sem, *,
                    rows_per_sample):
    b = pl.program_id(0)
    tile = b * pl.num_programs(1) + pl.program_id(1)
    first = first_ref[tile]
    last = last_ref[tile]
    r0 = (first // SUBLANES) * SUBLANES
    nch = jnp.where(last > first, (last - r0 + SEG_ROWS - 1) // SEG_ROWS, 0)

    def chunk_start(c):
        return pl.multiple_of(jnp.minimum(r0 + c * SEG_ROWS, rows_per_sample - SEG_ROWS), SUBLANES)

    def chunk_copy(c, slot):
        src = pl.multiple_of(b * rows_per_sample + chunk_start(c), SUBLANES)
        return pltpu.make_async_copy(y_hbm.at[pl.ds(src, SEG_ROWS), :], buf.at[slot], sem.at[slot])

    acc[...] = jnp.zeros(acc.shape, acc.dtype)

    @pl.when(nch > 0)
    def _():
        chunk_copy(0, 0).start()

    lo = lo_ref[0]
    hi = hi_ref[0]

    def body(c, carry):
        slot = c % 2

        @pl.when(c + 1 < nch)
        def _():
            chunk_copy(c + 1, 1 - slot).start()

        chunk_copy(c, slot).wait()
        row = (lax.broadcasted_iota(I32, (1, SEG_ROWS), 1) + chunk_start(c)).astype(F32)
        nominal = (r0 + c * SEG_ROWS).astype(F32)
        pick = jnp.logical_and(jnp.logical_and(lo <= row, row < hi), row >= nominal)
        pick = jnp.where(pick, 1.0, 0.0).astype(BF16)
        acc[...] += jnp.dot(pick, buf[slot].astype(BF16), preferred_element_type=F32)
        return carry

    lax.fori_loop(0, nch, body, 0)
    o_ref[0] = x_ref[0] + gt_ref[0] * (_rms(acc[...]) * g_ref[...])


def _combine(x1, y, seg, gate, g, tt=512):
    b, n, d = x1.shape
    rows_per_sample = y.shape[0] // b
    tt = min(tt, n)
    nj = n // tt
    seg_i = seg.astype(I32)
    first = seg_i[:, 0, ::tt].reshape(-1)
    last = seg_i[:, 1, tt - 1::tt].reshape(-1)
    rows = pl.BlockSpec((1, tt, d), lambda i, j, fr, lr: (i, j, 0))
    col = pl.BlockSpec((1, tt, 1), lambda i, j, fr, lr: (i, j, 0))
    return pl.pallas_call(
        functools.partial(_combine_kernel, rows_per_sample=rows_per_sample),
        grid_spec=pltpu.PrefetchScalarGridSpec(
            num_scalar_prefetch=2,
            grid=(b, nj),
            in_specs=[rows, col, col,
                      pl.BlockSpec((1, 1, d), lambda i, j, fr, lr: (i, 0, 0)),
                      pl.BlockSpec((1, d), lambda i, j, fr, lr: (0, 0)),
                      pl.BlockSpec(memory_space=pl.ANY)],
            out_specs=rows,
            scratch_shapes=[pltpu.VMEM((2, SEG_ROWS, d), F32), pltpu.VMEM((tt, d), F32),
                            pltpu.SemaphoreType.DMA((2,))],
        ),
        out_shape=jax.ShapeDtypeStruct((b, n, d), F32),
        compiler_params=_params("arbitrary", "arbitrary"),
        name="combine_residual",
    )(first, last, x1, seg[:, 0, :, None], seg[:, 1, :, None], gate.reshape(b, 1, d), g.reshape(1, d), y)


def _layer(x, c, ctx, c_ctx, p):
    b, n, d = x.shape
    lc = ctx.shape[1]
    rw = RET_HEADS * RET_HD
    lw = p["conv_w"].shape[1]
    q_off, k_off, v_off, g_off = 0, rw, 2 * rw, 3 * rw
    lx_off = 4 * rw
    ly_off = lx_off + lw
    gr_off = ly_off + lw

    cvecs = jnp.zeros((SUBLANES, d), F32).at[:b].set(c).at[b].set(c_ctx)
    mod = _ada_mod(cvecs, p["ada_w"], p["ada_b"])
    sh1, sc1, gt1, sh2, sc2, gt2 = [mod[:b, i * d:(i + 1) * d] for i in range(6)]
    csh1 = jnp.broadcast_to(mod[b:b + 1, :d], (b, d))
    csc1 = jnp.broadcast_to(mod[b:b + 1, d:2 * d], (b, d))

    w_in = p["w_in"]
    h = _prenorm(x, p["n_pre_mix"], sc1, sh1).reshape(b * n, d)
    hc = _prenorm(ctx, p["n_pre_mix"], csc1, csh1).reshape(b * lc, d)

    k_scale = RET_HD ** -0.5
    q_tabs, k_tabs = _rope_tables(n, k_scale)
    q = _proj(h, w_in, q_off, rw, "rope", tables=q_tabs, rows_per_seq=n)
    k = _proj(h, w_in, k_off, rw, "rope", tables=k_tabs, rows_per_seq=n)
    v = _proj(h, w_in, v_off, rw)
    g = _proj(h, w_in, g_off, rw, "silu")
    lx = _proj(h, w_in, lx_off, lw, out_dtype=F32)
    ly = _proj(h, w_in, ly_off, lw, "gelu")
    gates = _proj(h, w_in, gr_off, 2 * d, "sigmoid")
    kc = _proj(hc, w_in, k_off, rw, "scale", scale=k_scale)
    vc = _proj(hc, w_in, v_off, rw)
    lxc = _proj(hc, w_in, lx_off, lw, out_dtype=F32)

    of, ob = _retention(q.reshape(b, n, rw), k.reshape(b, n, rw), v.reshape(b, n, rw),
                        kc.reshape(b, lc, rw), vc.reshape(b, lc, rw), p["ret_lg"])

    wg = jnp.concatenate([p["gate_a_w"], p["gate_x_w"]], axis=-1).astype(BF16)
    hd = lw // LRU_HEADS
    bg = jnp.concatenate([p["gate_a_b"].reshape(2, LRU_HEADS, hd), p["gate_x_b"].reshape(2, LRU_HEADS, hd)], axis=-1)
    lru_w = (p["conv_w"], p["conv_b"], wg, bg, p["lam"])
    zero_state = jnp.zeros((b, lw), F32)
    hfc, hbc = _lru(lxc.reshape(b, lc, lw), *lru_w, zero_state, zero_state)
    hf, hb = _lru(lx.reshape(b, n, lw), *lru_w, hfc[:, -1], hbc[:, 0])

    a = _mergeprep(of.reshape(b * n, rw), ob.reshape(b * n, rw), g,
                   hf.reshape(b * n, lw), hb.reshape(b * n, lw), ly)
    merged = _merge(a, p["w_ret_out"], p["w_lru_out"], gates)
    mix = _proj(merged, p["w_out"], 0, d, out_dtype=F32)
    x1, h2, logits = _post_mix(x, mix.reshape(b, n, d), gt1, p["n_post_mix"], p["n_pre_ffn"], sc2, sh2,
                               p["router_w"].T.astype(BF16))

    ne = p["router_w"].shape[1]
    cap = EC_CAPACITY * n // ne
    idx, gate, dst, seg = _expert_choice(logits, cap)
    by_expert = lambda a: jnp.transpose(a[:, :, :ne], (2, 0, 1))
    sample = jnp.arange(b, dtype=I32)[None, :, None]
    src_rows = (by_expert(idx) + sample * n).reshape(-1)
    dst_rows = (by_expert(dst) + sample * (ne * cap)).reshape(-1)
    y = _expert_ffn(h2.reshape(b * n, d), p["w1"], p["w3"], p["w2"],
                    src_rows, dst_rows, by_expert(gate).reshape(-1, 1))
    return _combine(x1, y, seg, gt2, p["n_post_ffn"])


def kernel(x, c, ctx, c_ctx, ada_w, ada_b, norm_pre_mix, norm_post_mix, norm_pre_ffn, norm_post_ffn, w_in, ret_log_gamma, w_ret_out, lru_conv_w, lru_conv_b, lru_gate_a_w, lru_gate_a_b, lru_gate_x_w, lru_gate_x_b, lru_lambda, w_lru_out, w_out, router_w, expert_w1, expert_w3, expert_w2):
    depth = ada_w.shape[0]
    assert depth == 1, "context-stream update between layers is not implemented"
    l = 0
    p = {
        "ada_w": ada_w[l], "ada_b": ada_b[l],
        "n_pre_mix": norm_pre_mix[l], "n_post_mix": norm_post_mix[l],
        "n_pre_ffn": norm_pre_ffn[l], "n_post_ffn": norm_post_ffn[l],
        "w_in": w_in[l], "ret_lg": ret_log_gamma[l], "w_ret_out": w_ret_out[l],
        "conv_w": lru_conv_w[l], "conv_b": lru_conv_b[l],
        "gate_a_w": lru_gate_a_w[l], "gate_a_b": lru_gate_a_b[l],
        "gate_x_w": lru_gate_x_w[l], "gate_x_b": lru_gate_x_b[l], "lam": lru_lambda[l],
        "w_lru_out": w_lru_out[l], "w_out": w_out[l], "router_w": router_w[l],
        "w1": expert_w1[l], "w3": expert_w3[l], "w2": expert_w2[l],
    }
    return _layer(x, c, ctx, c_ctx, p)
```

```python
import functools

import jax
import jax.numpy as jnp
from jax import lax
from jax.experimental import pallas as pl
from jax.experimental.pallas import tpu as pltpu

F32 = jnp.float32
BF16 = jnp.bfloat16
I32 = jnp.int32

EPS = 1e-6
GRID_W = 64
RET_HEADS = 8
RET_HD = 256
ROPE_BASE = 10000.0
LRU_HEADS = 8
LRU_C = 8.0
CONV_W = 4
CONV_LEFT = 2
N_EXPERTS = 16
EC_CAPACITY = 2

LANES = 128
SUBLANES = 8
VMEM_LIMIT_BYTES = 60 * 1024 * 1024


def _params(*sem):
    return pltpu.CompilerParams(dimension_semantics=sem, vmem_limit_bytes=VMEM_LIMIT_BYTES)


def _rms(x):
    return x * lax.rsqrt(jnp.mean(x * x, axis=-1, keepdims=True) + EPS)


def _sigmoid(x):
    return 0.5 * jnp.tanh(0.5 * x) + 0.5


def _gelu_tanh(x):
    return 0.5 * x * (1.0 + jnp.tanh(0.7978845608028654 * (x + 0.044715 * (x * x * x))))


def _ada_kernel(c_ref, w_ref, b_ref, o_ref):
    c = c_ref[...]
    s = (c * _sigmoid(c)).astype(BF16)
    o_ref[...] = jnp.dot(s, w_ref[...].astype(BF16), preferred_element_type=F32) + b_ref[...]


def _ada_mod(cvecs, ada_w, ada_b, tn=512):
    d, w = ada_w.shape
    tn = min(tn, w)
    return pl.pallas_call(
        _ada_kernel,
        grid=(w // tn,),
        in_specs=[
            pl.BlockSpec((SUBLANES, d), lambda j: (0, 0)),
            pl.BlockSpec((d, tn), lambda j: (0, j)),
            pl.BlockSpec((1, tn), lambda j: (0, j)),
        ],
        out_specs=pl.BlockSpec((SUBLANES, tn), lambda j: (0, j)),
        out_shape=jax.ShapeDtypeStruct((SUBLANES, w), F32),
        compiler_params=_params("parallel"),
        name="ada_mod",
    )(cvecs, ada_w, ada_b.reshape(1, w))


def _prenorm_kernel(x_ref, g_ref, sc_ref, sh_ref, o_ref):
    y = _rms(x_ref[0]) * g_ref[...]
    o_ref[0] = (y * (1.0 + sc_ref[0]) + sh_ref[0]).astype(o_ref.dtype)


def _prenorm(x, g, scale, shift, tr=256):
    b, n, d = x.shape
    tr = min(tr, n)
    return pl.pallas_call(
        _prenorm_kernel,
        grid=(b, n // tr),
        in_specs=[
            pl.BlockSpec((1, tr, d), lambda i, j: (i, j, 0)),
            pl.BlockSpec((1, d), lambda i, j: (0, 0)),
            pl.BlockSpec((1, 1, d), lambda i, j: (i, 0, 0)),
            pl.BlockSpec((1, 1, d), lambda i, j: (i, 0, 0)),
        ],
        out_specs=pl.BlockSpec((1, tr, d), lambda i, j: (i, j, 0)),
        out_shape=jax.ShapeDtypeStruct((b, n, d), BF16),
        compiler_params=_params("parallel", "parallel"),
        name="prenorm",
    )(x, g.reshape(1, d), scale.reshape(b, 1, d), shift.reshape(b, 1, d))


def _proj_kernel(*refs, epilogue, scale):
    if epilogue == "rope":
        h_ref, w_ref, cos_ref, sin_ref, o_ref, wbf = refs
    else:
        h_ref, w_ref, o_ref, wbf = refs

    @pl.when(pl.program_id(1) == 0)
    def _():
        wbf[...] = w_ref[...].astype(BF16)

    acc = jnp.dot(h_ref[...], wbf[...], preferred_element_type=F32)
    if epilogue == "rope":
        for s in range(acc.shape[1] // LANES):
            a = acc[:, s * LANES:(s + 1) * LANES]
            ts = (s * LANES) % RET_HD
            cos = cos_ref[:, ts:ts + LANES]
            sin = sin_ref[:, ts:ts + LANES]
            o_ref[:, s * LANES:(s + 1) * LANES] = (
                a * cos + pltpu.roll(a, LANES // 2, 1) * sin).astype(o_ref.dtype)
        return
    if epilogue == "silu":
        acc = acc * _sigmoid(acc)
    elif epilogue == "gelu":
        acc = _gelu_tanh(acc)
    elif epilogue == "sigmoid":
        acc = _sigmoid(acc)
    elif epilogue == "scale":
        acc = acc * scale
    o_ref[...] = acc.astype(o_ref.dtype)


def _proj(h, w, col_off, width, epilogue="none", out_dtype=BF16, tables=None, rows_per_seq=None,
          scale=1.0, tm=1024, tn=512):
    m, k = h.shape
    tm = min(tm, m)
    tn = min(tn, width)
    if rows_per_seq is not None:
        tm = min(tm, rows_per_seq)
    jo = col_off // tn
    in_specs = [
        pl.BlockSpec((tm, k), lambda j, i: (i, 0)),
        pl.BlockSpec((k, tn), lambda j, i: (0, jo + j)),
    ]
    args = [h, w]
    if epilogue == "rope":
        nblk = rows_per_seq // tm
        tspec = pl.BlockSpec((tm, RET_HD), lambda j, i: (i % nblk, 0))
        in_specs += [tspec, tspec]
        args += list(tables)
    return pl.pallas_call(
        functools.partial(_proj_kernel, epilogue=epilogue, scale=scale),
        grid=(width // tn, m // tm),
        in_specs=in_specs,
        out_specs=pl.BlockSpec((tm, tn), lambda j, i: (i, j)),
        out_shape=jax.ShapeDtypeStruct((m, width), out_dtype),
        scratch_shapes=[pltpu.VMEM((k, tn), BF16)],
        compiler_params=_params("parallel", "arbitrary"),
        name="proj_" + epilogue,
    )(*args)


def _rope_tables(n, k_scale):
    quarter = RET_HD // 4
    freqs = ROPE_BASE ** (-jnp.arange(quarter, dtype=F32) / quarter)
    rows = n // GRID_W
    row = jnp.repeat(jnp.arange(rows, dtype=F32), GRID_W)
    col = jnp.tile(jnp.arange(GRID_W, dtype=F32), rows)

    def part(pos):
        ang = pos[:, None] * freqs[None, :]
        c, s = jnp.cos(ang), jnp.sin(ang)
        return jnp.concatenate([c, c], axis=-1), jnp.concatenate([-s, s], axis=-1)

    cr, sr = part(row)
    cc, sc = part(col)
    cos = jnp.concatenate([cr, cc], axis=-1)
    sin = jnp.concatenate([sr, sc], axis=-1)
    return (cos, sin), (cos * k_scale, sin * k_scale)


def _dot_nt(a, b):
    return lax.dot_general(a, b, (((1,), (1,)), ((), ())), preferred_element_type=F32)


def _dot_tn(a, b):
    return lax.dot_general(a, b, (((0,), (0,)), ((), ())), preferred_element_type=F32)


def _ret_kernel(lg_ref, qf_ref, kf_ref, vf_ref, qb_ref, kb_ref, vb_ref, kc_ref, vc_ref,
                of_ref, ob_ref, sf_ref, sb_ref, *, csz):
    h = pl.program_id(1)
    c = pl.program_id(2)
    lgf = lg_ref[0, h]
    lgb = lg_ref[1, h]
    nsub = qf_ref.shape[1] // csz

    @pl.when(c == 0)
    def _():
        kc = kc_ref[0].astype(F32)
        vc = vc_ref[0]
        lc = kc.shape[0]
        m = lax.broadcasted_iota(I32, (lc, 1), 0).astype(F32)
        sf_ref[...] = _dot_tn((kc * jnp.exp((lc - 1.0 - m) * lgf)).astype(BF16), vc)
        sb_ref[...] = _dot_tn((kc * jnp.exp(m * lgb)).astype(BF16), vc)

    qi = lax.broadcasted_iota(I32, (csz, csz), 0)
    ki = lax.broadcasted_iota(I32, (csz, csz), 1)
    j = lax.broadcasted_iota(I32, (csz, 1), 0).astype(F32)

    def direction(q_ref, k_ref, v_ref, o_ref, s_ref, lg, dist, q_pow, k_pow, order):
        intra = jnp.where(dist >= 0, jnp.exp(jnp.maximum(dist, 0).astype(F32) * lg), 0.0)
        q_dec = jnp.exp(q_pow * lg)
        k_dec = jnp.exp(k_pow * lg)
        chunk_dec = jnp.exp(jnp.full((1, 1), csz, F32) * lg)
        state = s_ref[...]
        for sub in order:
            rows = slice(sub * csz, (sub + 1) * csz)
            q = q_ref[0, rows, :]
            k = k_ref[0, rows, :]
            v = v_ref[0, rows, :]
            s = _dot_nt(q, k) * intra
            o = jnp.dot(s.astype(BF16), v, preferred_element_type=F32)
            o_ref[0, rows, :] = o + jnp.dot(q, state.astype(BF16), preferred_element_type=F32) * q_dec
            kd = (k.astype(F32) * k_dec).astype(BF16)
            state = state * chunk_dec + _dot_tn(kd, v)
        s_ref[...] = state

    direction(qf_ref, kf_ref, vf_ref, of_ref, sf_ref, lgf, qi - ki, j + 1.0, csz - 1.0 - j, range(nsub))
    direction(qb_ref, kb_ref, vb_ref, ob_ref, sb_ref, lgb, ki - qi, csz - j, j, range(nsub - 1, -1, -1))


def _retention(q, k, v, kc, vc, log_gamma, chunk=256, block=1024):
    b, n, hw = q.shape
    dh = RET_HD
    nh = hw // dh
    lc = kc.shape[1]
    chunk = min(chunk, n)
    block = min(block, n)
    nc = n // block
    fwd = pl.BlockSpec((1, block, dh), lambda bi, hi, ci: (bi, ci, hi))
    bwd = pl.BlockSpec((1, block, dh), lambda bi, hi, ci: (bi, nc - 1 - ci, hi))
    ctx = pl.BlockSpec((1, lc, dh), lambda bi, hi, ci: (bi, 0, hi))
    return pl.pallas_call(
        functools.partial(_ret_kernel, csz=chunk),
        grid=(b, nh, nc),
        in_specs=[pl.BlockSpec(memory_space=pltpu.SMEM), fwd, fwd, fwd, bwd, bwd, bwd, ctx, ctx],
        out_specs=[fwd, bwd],
        out_shape=[jax.ShapeDtypeStruct((b, n, hw), F32)] * 2,
        scratch_shapes=[pltpu.VMEM((dh, dh), F32), pltpu.VMEM((dh, dh), F32)],
        compiler_params=_params("parallel", "parallel", "arbitrary"),
        name="retention",
    )(log_gamma.astype(F32), q, k, v, q, k, v, kc, vc)


HALO = SUBLANES
SCAN_UNROLL = 8


def _lru_kernel(mf_ref, pf_ref, nf_ref, mb_ref, pb_ref, nb_ref, cw_ref, cb_ref, wg_ref, bg_ref, lam_ref,
                h0f_ref, h0b_ref, hf_ref, hb_ref, ext_s, a_s, b_s, cf_s, cb_s):
    t = pl.program_id(2)
    nt = pl.num_programs(2)
    tt = mf_ref.shape[1]
    w = mf_ref.shape[2]

    @pl.when(t == 0)
    def _():
        cf_s[...] = h0f_ref[0]
        cb_s[...] = h0b_ref[0]

    def coeffs(m_ref, p_ref, n_ref, has_prev, has_next, d):
        ext_s[0:HALO, :] = jnp.where(has_prev, p_ref[0], 0.0)
        ext_s[HALO:HALO + tt, :] = m_ref[0]
        ext_s[HALO + tt:, :] = jnp.where(has_next, n_ref[0], 0.0)
        u = cb_ref[...]
        for i in range(CONV_W):
            off = HALO + i - CONV_LEFT
            u = u + ext_s[off:off + tt, :] * cw_ref[i:i + 1, :]
        g = jnp.dot(u.astype(BF16), wg_ref[d, 0], preferred_element_type=F32) + bg_ref[d:d + 1, :]
        r = _sigmoid(g[:, :w])
        ig = _sigmoid(g[:, w:])
        lam = lam_ref[d:d + 1, :]
        softplus = jnp.maximum(-lam, 0.0) + jnp.log(1.0 + jnp.exp(-jnp.abs(lam)))
        log_a = -LRU_C * r * softplus
        a = jnp.exp(log_a)
        a_s[...] = a
        b_s[...] = jnp.sqrt(-jnp.tanh(log_a) * (a * a + 1.0)) * (ig * u)

    row = lax.broadcasted_iota(I32, (SUBLANES, w), 0)
    ntile = tt // SUBLANES

    coeffs(mf_ref, pf_ref, nf_ref, t > 0, t < nt - 1, 0)

    def fwd_body(i, carry):
        r0 = pl.multiple_of(i * SUBLANES, SUBLANES)
        a = a_s[pl.ds(r0, SUBLANES), :]
        b = b_s[pl.ds(r0, SUBLANES), :]
        for s in (1, 2, 4):
            keep = row >= s
            b = jnp.where(keep, a * pltpu.roll(b, s, 0) + b, b)
            a = jnp.where(keep, a * pltpu.roll(a, s, 0), a)
        hcur = a * carry + b
        hf_ref[0, pl.ds(r0, SUBLANES), :] = hcur
        return hcur[SUBLANES - 1:SUBLANES, :]

    cf_s[...] = lax.fori_loop(0, ntile, fwd_body, cf_s[...], unroll=min(SCAN_UNROLL, ntile))

    coeffs(mb_ref, pb_ref, nb_ref, t < nt - 1, t > 0, 1)

    def bwd_body(i, carry):
        r0 = pl.multiple_of((ntile - 1 - i) * SUBLANES, SUBLANES)
        a = a_s[pl.ds(r0, SUBLANES), :]
        b = b_s[pl.ds(r0, SUBLANES), :]
        for s in (1, 2, 4):
            keep = row < SUBLANES - s
            b = jnp.where(keep, a * pltpu.roll(b, SUBLANES - s, 0) + b, b)
            a = jnp.where(keep, a * pltpu.roll(a, SUBLANES - s, 0), a)
        hcur = a * carry + b
        hb_ref[0, pl.ds(r0, SUBLANES), :] = hcur
        return hcur[0:1, :]

    cb_s[...] = lax.fori_loop(0, ntile, bwd_body, cb_s[...], unroll=min(SCAN_UNROLL, ntile))


def _lru(lx, conv_w, conv_b, wg, bg, lam, h0f, h0b, tt=512):
    b, n, wtot = lx.shape
    nh = wg.shape[1]
    hd = wtot // nh
    tt = min(tt, n)
    nt = n // tt
    hb_per = tt // HALO
    nhalo = n // HALO

    def main_spec(rev):
        return pl.BlockSpec((1, tt, hd), lambda bi, hi, ti: (bi, (nt - 1 - ti) if rev else ti, hi))

    def prev_spec(rev):
        return pl.BlockSpec(
            (1, HALO, hd),
            lambda bi, hi, ti: (bi, jnp.maximum(((nt - 1 - ti) if rev else ti) * hb_per - 1, 0), hi))

    def next_spec(rev):
        return pl.BlockSpec(
            (1, HALO, hd),
            lambda bi, hi, ti: (bi, jnp.minimum((((nt - 1 - ti) if rev else ti) + 1) * hb_per, nhalo - 1), hi))

    head_vec = lambda rows: pl.BlockSpec((rows, hd), lambda bi, hi, ti: (0, hi))
    state = pl.BlockSpec((1, 1, hd), lambda bi, hi, ti: (bi, 0, hi))
    bg2 = bg.reshape(2, nh * 2 * hd)
    return pl.pallas_call(
        _lru_kernel,
        grid=(b, nh, nt),
        in_specs=[
            main_spec(False), prev_spec(False), next_spec(False),
            main_spec(True), prev_spec(True), next_spec(True),
            head_vec(CONV_W), head_vec(1),
            pl.BlockSpec((2, 1, hd, 2 * hd), lambda bi, hi, ti: (0, hi, 0, 0)),
            pl.BlockSpec((2, 2 * hd), lambda bi, hi, ti: (0, hi)),
            head_vec(2), state, state,
        ],
        out_specs=[main_spec(False), main_spec(True)],
        out_shape=[jax.ShapeDtypeStruct((b, n, wtot), F32)] * 2,
        scratch_shapes=[pltpu.VMEM((tt + 2 * HALO, hd), F32), pltpu.VMEM((tt, hd), F32), pltpu.VMEM((tt, hd), F32),
                        pltpu.VMEM((1, hd), F32), pltpu.VMEM((1, hd), F32)],
        compiler_params=_params("parallel", "parallel", "arbitrary"),
        name="rglru",
    )(lx, lx, lx, lx, lx, lx, conv_w, conv_b.reshape(1, wtot), wg, bg2, lam,
      h0f.reshape(b, 1, wtot), h0b.reshape(b, 1, wtot))


def _mergeprep_kernel(of_ref, ob_ref, g_ref, hf_ref, hb_ref, ly_ref, o_ref):
    rw = of_ref.shape[1]
    for h in range(rw // RET_HD):
        sl = slice(h * RET_HD, (h + 1) * RET_HD)
        o = _rms(of_ref[:, sl] + ob_ref[:, sl])
        o_ref[:, sl] = (o * g_ref[:, sl].astype(F32)).astype(o_ref.dtype)
    o_ref[:, rw:] = (ly_ref[...].astype(F32) * (hf_ref[...] + hb_ref[...])).astype(o_ref.dtype)


def _mergeprep(of, ob, g, hf, hb, ly, tr=256):
    m, rw = of.shape
    lw = hf.shape[1]
    tr = min(tr, m)
    spec = lambda wd: pl.BlockSpec((tr, wd), lambda i: (i, 0))
    return pl.pallas_call(
        _mergeprep_kernel,
        grid=(m // tr,),
        in_specs=[spec(rw), spec(rw), spec(rw), spec(lw), spec(lw), spec(lw)],
        out_specs=spec(rw + lw),
        out_shape=jax.ShapeDtypeStruct((m, rw + lw), BF16),
        compiler_params=_params("parallel"),
        name="mergeprep",
    )(of, ob, g, hf, hb, ly)


def _merge_kernel(a_ref, wr_ref, wl_ref, gr_ref, gl_ref, o_ref, wr_bf, wl_bf):
    rw = wr_ref.shape[0]

    @pl.when(pl.program_id(1) == 0)
    def _():
        wr_bf[...] = wr_ref[...].astype(BF16)
        wl_bf[...] = wl_ref[...].astype(BF16)

    ret = jnp.dot(a_ref[:, :rw], wr_bf[...], preferred_element_type=F32)
    lru = jnp.dot(a_ref[:, rw:], wl_bf[...], preferred_element_type=F32)
    o_ref[...] = (gr_ref[...].astype(F32) * ret + gl_ref[...].astype(F32) * lru).astype(o_ref.dtype)


def _merge(a, w_ret_out, w_lru_out, gates, tm=1024, tn=512):
    m = a.shape[0]
    rw, d = w_ret_out.shape
    lw = w_lru_out.shape[0]
    tm = min(tm, m)
    tn = min(tn, d)
    nj = d // tn
    return pl.pallas_call(
        _merge_kernel,
        grid=(nj, m // tm),
        in_specs=[
            pl.BlockSpec((tm, rw + lw), lambda j, i: (i, 0)),
            pl.BlockSpec((rw, tn), lambda j, i: (0, j)),
            pl.BlockSpec((lw, tn), lambda j, i: (0, j)),
            pl.BlockSpec((tm, tn), lambda j, i: (i, j)),
            pl.BlockSpec((tm, tn), lambda j, i: (i, nj + j)),
        ],
        out_specs=pl.BlockSpec((tm, tn), lambda j, i: (i, j)),
        out_shape=jax.ShapeDtypeStruct((m, d), BF16),
        scratch_shapes=[pltpu.VMEM((rw, tn), BF16), pltpu.VMEM((lw, tn), BF16)],
        compiler_params=_params("parallel", "arbitrary"),
        name="merge",
    )(a, w_ret_out, w_lru_out, gates, gates)


def _post_mix_kernel(x_ref, mix_ref, gt_ref, g1_ref, g2_ref, sc_ref, sh_ref, rw_ref, x1_ref, h2_ref, lg_ref):
    x1 = x_ref[0] + gt_ref[0] * (_rms(mix_ref[0]) * g1_ref[...])
    x1_ref[0] = x1
    h2 = (_rms(x1) * g2_ref[...]) * (1.0 + sc_ref[0]) + sh_ref[0]
    h2_ref[0] = h2
    lg_ref[0] = _dot_nt(rw_ref[...], h2.astype(BF16))


def _post_mix(x, mix, gate, g_post, g_pre, scale, shift, router_w_t, tr=256):
    b, n, d = x.shape
    e = router_w_t.shape[0]
    tr = min(tr, n)
    rows = pl.BlockSpec((1, tr, d), lambda i, j: (i, j, 0))
    vec = pl.BlockSpec((1, d), lambda i, j: (0, 0))
    bvec = pl.BlockSpec((1, 1, d), lambda i, j: (i, 0, 0))
    return pl.pallas_call(
        _post_mix_kernel,
        grid=(b, n // tr),
        in_specs=[rows, rows, bvec, vec, vec, bvec, bvec, pl.BlockSpec((e, d), lambda i, j: (0, 0))],
        out_specs=[rows, rows, pl.BlockSpec((1, e, tr), lambda i, j: (i, 0, j))],
        out_shape=[jax.ShapeDtypeStruct((b, n, d), F32), jax.ShapeDtypeStruct((b, n, d), F32),
                   jax.ShapeDtypeStruct((b, e, n), F32)],
        compiler_params=_params("parallel", "parallel"),
        name="post_mix",
    )(x, mix, gate.reshape(b, 1, d), g_post.reshape(1, d), g_pre.reshape(1, d),
      scale.reshape(b, 1, d), shift.reshape(b, 1, d), router_w_t)


def _lane_cumsum_excl(m):
    e, n = m.shape
    tri = (lax.broadcasted_iota(I32, (LANES, LANES), 0) < lax.broadcasted_iota(I32, (LANES, LANES), 1)).astype(BF16)
    carry = jnp.zeros((e, 1), F32)
    outs = []
    for c in range(n // LANES):
        blk = m[:, c * LANES:(c + 1) * LANES]
        outs.append(jnp.dot(blk.astype(BF16), tri, preferred_element_type=F32) + carry)
        carry = carry + jnp.sum(blk, axis=1, keepdims=True)
    return jnp.concatenate(outs, axis=1)


SEL_SLOTS = 256
SEL_TOKENS = 512


def _sel_kernel(lg_ref, idx_ref, gate_ref, dst_ref, seg_ref, aff_s, pos_s, dst_s, starts_v, starts_sm, sem, *, cap):
    lg = lg_ref[0]
    ne, n = lg.shape
    ex = jnp.exp(lg - jnp.max(lg, axis=0, keepdims=True))
    aff = ex / jnp.sum(ex, axis=0, keepdims=True)
    thr_bits = jnp.zeros((ne, 1), I32)
    for bit in range(30, -1, -1):
        cand = thr_bits | (1 << bit)
        cnt = jnp.sum((aff >= lax.bitcast_convert_type(cand, F32)).astype(I32), axis=1, keepdims=True)
        thr_bits = jnp.where(cnt >= cap, cand, thr_bits)
    thr = lax.bitcast_convert_type(thr_bits, F32)

    above = aff > thr
    tied = (aff == thr).astype(F32)
    need = (cap - jnp.sum(above.astype(I32), axis=1, keepdims=True)).astype(F32)
    sel = jnp.where(above, 1.0, jnp.where(_lane_cumsum_excl(tied) < need, tied, 0.0))

    per_token = jnp.sum(sel, axis=0, keepdims=True)
    seg_lo = _lane_cumsum_excl(jnp.broadcast_to(per_token, (ne, n)))[0:1, :]
    seg_ref[0, 0:1, :] = seg_lo
    seg_ref[0, 1:2, :] = seg_lo + per_token
    taken = seg_lo
    for e in range(ne):
        dst_s[e:e + 1, :] = taken
        taken = taken + sel[e:e + 1, :]

    aff_s[...] = aff
    before = _lane_cumsum_excl(sel)
    pos_s[...] = (before + 1.0) * sel

    ss = min(SEL_SLOTS, cap)
    st = min(SEL_TOKENS, n)
    nchunk = n // st

    lane_e = lax.broadcasted_iota(I32, (ne, LANES), 1)
    starts = jnp.full((ne, LANES), cap, I32)
    for c in range(nchunk):
        starts = jnp.where(lane_e == c, before[:, c * st:c * st + 1].astype(I32), starts)
    starts_v[...] = starts
    to_smem = pltpu.make_async_copy(starts_v, starts_sm, sem)
    to_smem.start()
    to_smem.wait()

    lane = lax.broadcasted_iota(I32, (ss, LANES), 1)
    s_base = lax.broadcasted_iota(I32, (ss, st), 0).astype(F32) + 1.0
    t_base = lax.broadcasted_iota(I32, (1, st), 1).astype(F32)
    idx_ref[...] = jnp.zeros(idx_ref.shape, idx_ref.dtype)
    gate_ref[...] = jnp.zeros(gate_ref.shape, gate_ref.dtype)
    dst_ref[...] = jnp.zeros(dst_ref.shape, dst_ref.dtype)

    def fold(v):
        out = v[:, :LANES]
        for q in range(1, st // LANES):
            out = out + v[:, q * LANES:(q + 1) * LANES]
        return out

    def per_expert(e, _):
        for sb in range(cap // ss):
            sv = s_base + float(sb * ss)

            def per_chunk(c, carry):
                ia, ga, da = carry
                t0 = pl.multiple_of(c * st, st)
                hit = pos_s[pl.ds(e, 1), pl.ds(t0, st)] == sv
                tok = t_base + lax.convert_element_type(c * st, F32)
                ia = ia + fold(jnp.where(hit, tok, 0.0))
                ga = ga + fold(jnp.where(hit, aff_s[pl.ds(e, 1), pl.ds(t0, st)], 0.0))
                da = da + fold(jnp.where(hit, dst_s[pl.ds(e, 1), pl.ds(t0, st)], 0.0))
                return ia, ga, da

            c_lo = jnp.int32(0)
            c_hi = jnp.int32(0)
            for c in range(nchunk):
                c_lo = c_lo + (starts_sm[e, c + 1] <= sb * ss).astype(I32)
                c_hi = c_hi + (starts_sm[e, c] < (sb + 1) * ss).astype(I32)
            zero = jnp.zeros((ss, LANES), F32)
            ia, ga, da = lax.fori_loop(c_lo, c_hi, per_chunk, (zero, zero, zero))
            rows = slice(sb * ss, (sb + 1) * ss)
            mine = lane == e
            for ref, acc in ((idx_ref, ia), (gate_ref, ga), (dst_ref, da)):
                col = jnp.sum(acc, axis=1, keepdims=True).astype(ref.dtype)
                ref[0, rows, :] = jnp.where(mine, col, ref[0, rows, :])
        return 0

    lax.fori_loop(0, ne, per_expert, 0)


def _expert_choice(logits, cap):
    b, ne, n = logits.shape
    slot = pl.BlockSpec((1, cap, LANES), lambda i: (i, 0, 0))
    return pl.pallas_call(
        functools.partial(_sel_kernel, cap=cap),
        grid=(b,),
        in_specs=[pl.BlockSpec((1, ne, n), lambda i: (i, 0, 0))],
        out_specs=[slot, slot, slot, pl.BlockSpec((1, 2, n), lambda i: (i, 0, 0))],
        out_shape=[jax.ShapeDtypeStruct((b, cap, LANES), I32), jax.ShapeDtypeStruct((b, cap, LANES), F32),
                   jax.ShapeDtypeStruct((b, cap, LANES), I32), jax.ShapeDtypeStruct((b, 2, n), F32)],
        scratch_shapes=[pltpu.VMEM((ne, n), F32)] * 3 + [pltpu.VMEM((ne, LANES), I32), pltpu.SMEM((ne, LANES), I32),
                                                       pltpu.SemaphoreType.DMA],
        compiler_params=_params("arbitrary"),
        name="expert_choice",
    )(logits)


def _row_copy(src, dst, src_row, dst_row, sem):
    return pltpu.make_async_copy(src.at[pl.ds(src_row, 1), :], dst.at[pl.ds(dst_row, 1), :], sem)


def _ffn_kernel(src_ref, dst_ref, h_hbm, w1_ref, w3_ref, w2_ref, gate_ref, y_hbm, xin, xbf, acc, gsem, psem, *, nf):
    tm, d = xbf.shape
    per_step = tm // nf
    i = pl.program_id(0)
    f = pl.program_id(1)
    nt = pl.num_programs(0)
    cur = i % 2
    oth = 1 - cur

    def gather(tile, r):
        return _row_copy(h_hbm, xin, src_ref[tile * tm + r], r, gsem)

    def put(tile, slot, r):
        return _row_copy(acc.at[slot], y_hbm, r, dst_ref[tile * tm + r], psem.at[slot])

    def all_gathered():
        pltpu.make_async_copy(h_hbm.at[pl.ds(0, tm), :], xin, gsem).wait()

    def all_put(slot):
        pltpu.make_async_copy(acc.at[slot], y_hbm.at[pl.ds(0, tm), :], psem.at[slot]).wait()

    def for_rows(fn):
        def body(r, carry):
            fn(r)
            return carry
        lax.fori_loop(0, tm, body, 0)

    @pl.when(jnp.logical_and(i == 0, f == 0))
    def _():
        for_rows(lambda r: gather(0, r).start())

    @pl.when(f == 0)
    def _():
        all_gathered()
        xbf[...] = xin[...].astype(BF16)
        acc[cur] = jnp.zeros((tm, d), F32)

    nxt = jnp.where(i + 1 < nt, i + 1, 0)

    def step(with_puts):
        for r in range(per_step):
            gather(nxt, f * per_step + r).start()
        if with_puts:
            for r in range(per_step):
                put(i - 1, oth, f * per_step + r).start()
        x = xbf[...]
        h1 = jnp.dot(x, w1_ref[0].astype(BF16), preferred_element_type=F32)
        h3 = jnp.dot(x, w3_ref[0].astype(BF16), preferred_element_type=F32)
        hid = (h1 * _sigmoid(h1) * h3).astype(BF16)
        acc[cur] += jnp.dot(hid, w2_ref[0].astype(BF16), preferred_element_type=F32)

    lax.cond(i > 0, lambda: step(True), lambda: step(False))

    @pl.when(f == nf - 1)
    def _():
        acc[cur] = acc[cur] * gate_ref[...]

        @pl.when(i > 0)
        def _():
            all_put(oth)

        @pl.when(i == nt - 1)
        def _():
            for_rows(lambda r: put(i, cur, r).start())
            all_put(cur)
            all_gathered()


def _expert_ffn(h2, w1, w3, w2, src_rows, dst_rows, gate, tm=512, tf=256):
    ne, d, ff = w1.shape
    total = src_rows.shape[0]
    s = total // ne
    tm = min(tm, s)
    tf = min(tf, ff)
    nf = ff // tf
    per_e = s // tm
    assert tm % nf == 0, "row copies are issued in equal shares per grid step"
    return pl.pallas_call(
        functools.partial(_ffn_kernel, nf=nf),
        grid_spec=pltpu.PrefetchScalarGridSpec(
            num_scalar_prefetch=2,
            grid=(ne * per_e, nf),
            in_specs=[
                pl.BlockSpec(memory_space=pl.ANY),
                pl.BlockSpec((1, d, tf), lambda i, f, sr, dr: (i // per_e, 0, f)),
                pl.BlockSpec((1, d, tf), lambda i, f, sr, dr: (i // per_e, 0, f)),
                pl.BlockSpec((1, tf, d), lambda i, f, sr, dr: (i // per_e, f, 0)),
                pl.BlockSpec((tm, 1), lambda i, f, sr, dr: (i, 0)),
            ],
            out_specs=pl.BlockSpec(memory_space=pl.ANY),
            scratch_shapes=[pltpu.VMEM((tm, d), F32), pltpu.VMEM((tm, d), BF16), pltpu.VMEM((2, tm, d), F32),
                            pltpu.SemaphoreType.DMA, pltpu.SemaphoreType.DMA((2,))],
        ),
        out_shape=jax.ShapeDtypeStruct((total, d), F32),
        compiler_params=_params("arbitrary", "arbitrary"),
        name="expert_ffn",
    )(src_rows, dst_rows, h2, w1, w3, w2, gate)


SEG_ROWS = 256


def _combine_kernel(first_ref, last_ref, x_ref, lo_ref, hi_ref, gt_ref, g_ref, y_hbm, o_ref, buf, acc, sem, *,
                    rows_per_sample):
    b = pl.program_id(0)
    tile = b * pl.num_programs(1) + pl.program_id(1)
    first = first_ref[tile]
    last = last_ref[tile]
    r0 = (first // SUBLANES) * SUBLANES
    nch = jnp.where(last > first, (last - r0 + SEG_ROWS - 1) // SEG_ROWS, 0)

    def chunk_start(c):
        return pl.multiple_of(jnp.minimum(r0 + c * SEG_ROWS, rows_per_sample - SEG_ROWS), SUBLANES)

    def chunk_copy(c, slot):
        src = pl.multiple_of(b * rows_per_sample + chunk_start(c), SUBLANES)
        return pltpu.make_async_copy(y_hbm.at[pl.ds(src, SEG_ROWS), :], buf.at[slot], sem.at[slot])

    acc[...] = jnp.zeros(acc.shape, acc.dtype)

    @pl.when(nch > 0)
    def _():
        chunk_copy(0, 0).start()

    lo = lo_ref[0]
    hi = hi_ref[0]

    def body(c, carry):
        slot = c % 2

        @pl.when(c + 1 < nch)
        def _():
            chunk_copy(c + 1, 1 - slot).start()

        chunk_copy(c, slot).wait()
        row = (lax.broadcasted_iota(I32, (1, SEG_ROWS), 1) + chunk_start(c)).astype(F32)
        nominal = (r0 + c * SEG_ROWS).astype(F32)
        pick = jnp.logical_and(jnp.logical_and(lo <= row, row < hi), row >= nominal)
        pick = jnp.where(pick, 1.0, 0.0).astype(BF16)
        acc[...] += jnp.dot(pick, buf[slot].astype(BF16), preferred_element_type=F32)
        return carry

    lax.fori_loop(0, nch, body, 0)
    o_ref[0] = x_ref[0] + gt_ref[0] * (_rms(acc[...]) * g_ref[...])


def _combine(x1, y, seg, gate, g, tt=512):
    b, n, d = x1.shape
    rows_per_sample = y.shape[0] // b
    tt = min(tt, n)
    nj = n // tt
    seg_i = seg.astype(I32)
    first = seg_i[:, 0, ::tt].reshape(-1)
    last = seg_i[:, 1, tt - 1::tt].reshape(-1)
    rows = pl.BlockSpec((1, tt, d), lambda i, j, fr, lr: (i, j, 0))
    col = pl.BlockSpec((1, tt, 1), lambda i, j, fr, lr: (i, j, 0))
    return pl.pallas_call(
        functools.partial(_combine_kernel, rows_per_sample=rows_per_sample),
        grid_spec=pltpu.PrefetchScalarGridSpec(
            num_scalar_prefetch=2,
            grid=(b, nj),
            in_specs=[rows, col, col,
                      pl.BlockSpec((1, 1, d), lambda i, j, fr, lr: (i, 0, 0)),
                      pl.BlockSpec((1, d), lambda i, j, fr, lr: (0, 0)),
                      pl.BlockSpec(memory_space=pl.ANY)],
            out_specs=rows,
            scratch_shapes=[pltpu.VMEM((2, SEG_ROWS, d), F32), pltpu.VMEM((tt, d), F32),
                            pltpu.SemaphoreType.DMA((2,))],
        ),
        out_shape=jax.ShapeDtypeStruct((b, n, d), F32),
        compiler_params=_params("arbitrary", "arbitrary"),
        name="combine_residual",
    )(first, last, x1, seg[:, 0, :, None], seg[:, 1, :, None], gate.reshape(b, 1, d), g.reshape(1, d), y)


def _layer(x, c, ctx, c_ctx, p):
    b, n, d = x.shape
    lc = ctx.shape[1]
    rw = RET_HEADS * RET_HD
    lw = p["conv_w"].shape[1]
    q_off, k_off, v_off, g_off = 0, rw, 2 * rw, 3 * rw
    lx_off = 4 * rw
    ly_off = lx_off + lw
    gr_off = ly_off + lw

    cvecs = jnp.zeros((SUBLANES, d), F32).at[:b].set(c).at[b].set(c_ctx)
    mod = _ada_mod(cvecs, p["ada_w"], p["ada_b"])
    sh1, sc1, gt1, sh2, sc2, gt2 = [mod[:b, i * d:(i + 1) * d] for i in range(6)]
    csh1 = jnp.broadcast_to(mod[b:b + 1, :d], (b, d))
    csc1 = jnp.broadcast_to(mod[b:b + 1, d:2 * d], (b, d))

    w_in = p["w_in"]
    h = _prenorm(x, p["n_pre_mix"], sc1, sh1).reshape(b * n, d)
    hc = _prenorm(ctx, p["n_pre_mix"], csc1, csh1).reshape(b * lc, d)

    k_scale = RET_HD ** -0.5
    q_tabs, k_tabs = _rope_tables(n, k_scale)
    q = _proj(h, w_in, q_off, rw, "rope", tables=q_tabs, rows_per_seq=n)
    k = _proj(h, w_in, k_off, rw, "rope", tables=k_tabs, rows_per_seq=n)
    v = _proj(h, w_in, v_off, rw)
    g = _proj(h, w_in, g_off, rw, "silu")
    lx = _proj(h, w_in, lx_off, lw, out_dtype=F32)
    ly = _proj(h, w_in, ly_off, lw, "gelu")
    gates = _proj(h, w_in, gr_off, 2 * d, "sigmoid")
    kc = _proj(hc, w_in, k_off, rw, "scale", scale=k_scale)
    vc = _proj(hc, w_in, v_off, rw)
    lxc = _proj(hc, w_in, lx_off, lw, out_dtype=F32)

    of, ob = _retention(q.reshape(b, n, rw), k.reshape(b, n, rw), v.reshape(b, n, rw),
                        kc.reshape(b, lc, rw), vc.reshape(b, lc, rw), p["ret_lg"])

    wg = jnp.concatenate([p["gate_a_w"], p["gate_x_w"]], axis=-1).astype(BF16)
    hd = lw // LRU_HEADS
    bg = jnp.concatenate([p["gate_a_b"].reshape(2, LRU_HEADS, hd), p["gate_x_b"].reshape(2, LRU_HEADS, hd)], axis=-1)
    lru_w = (p["conv_w"], p["conv_b"], wg, bg, p["lam"])
    zero_state = jnp.zeros((b, lw), F32)
    hfc, hbc = _lru(lxc.reshape(b, lc, lw), *lru_w, zero_state, zero_state)
    hf, hb = _lru(lx.reshape(b, n, lw), *lru_w, hfc[:, -1], hbc[:, 0])

    a = _mergeprep(of.reshape(b * n, rw), ob.reshape(b * n, rw), g,
                   hf.reshape(b * n, lw), hb.reshape(b * n, lw), ly)
    merged = _merge(a, p["w_ret_out"], p["w_lru_out"], gates)
    mix = _proj(merged, p["w_out"], 0, d, out_dtype=F32)
    x1, h2, logits = _post_mix(x, mix.reshape(b, n, d), gt1, p["n_post_mix"], p["n_pre_ffn"], sc2, sh2,
                               p["router_w"].T.astype(BF16))

    ne = p["router_w"].shape[1]
    cap = EC_CAPACITY * n // ne
    idx, gate, dst, seg = _expert_choice(logits, cap)
    by_expert = lambda a: jnp.transpose(a[:, :, :ne], (2, 0, 1))
    sample = jnp.arange(b, dtype=I32)[None, :, None]
    src_rows = (by_expert(idx) + sample * n).reshape(-1)
    dst_rows = (by_expert(dst) + sample * (ne * cap)).reshape(-1)
    y = _expert_ffn(h2.reshape(b * n, d), p["w1"], p["w3"], p["w2"],
                    src_rows, dst_rows, by_expert(gate).reshape(-1, 1))
    return _combine(x1, y, seg, gt2, p["n_post_ffn"])


def kernel(x, c, ctx, c_ctx, ada_w, ada_b, norm_pre_mix, norm_post_mix, norm_pre_ffn, norm_post_ffn, w_in, ret_log_gamma, w_ret_out, lru_conv_w, lru_conv_b, lru_gate_a_w, lru_gate_a_b, lru_gate_x_w, lru_gate_x_b, lru_lambda, w_lru_out, w_out, router_w, expert_w1, expert_w3, expert_w2):
    depth = ada_w.shape[0]
    assert depth == 1, "context-stream update between layers is not implemented"
    l = 0
    p = {
        "ada_w": ada_w[l], "ada_b": ada_b[l],
        "n_pre_mix": norm_pre_mix[l], "n_post_mix": norm_post_mix[l],
        "n_pre_ffn": norm_pre_ffn[l], "n_post_ffn": norm_post_ffn[l],
        "w_in": w_in[l], "ret_lg": ret_log_gamma[l], "w_ret_out": w_ret_out[l],
        "conv_w": lru_conv_w[l], "conv_b": lru_conv_b[l],
        "gate_a_w": lru_gate_a_w[l], "gate_a_b": lru_gate_a_b[l],
        "gate_x_w": lru_gate_x_w[l], "gate_x_b": lru_gate_x_b[l], "lam": lru_lambda[l],
        "w_lru_out": w_lru_out[l], "w_out": w_out[l], "router_w": router_w[l],
        "w1": expert_w1[l], "w3": expert_w3[l], "w2": expert_w2[l],
    }
    return _layer(x, c, ctx, c_ctx, p)
```

```python
import functools

import jax
import jax.numpy as jnp
from jax import lax
from jax.experimental import pallas as pl
from jax.experimental.pallas import tpu as pltpu

F32 = jnp.float32
BF16 = jnp.bfloat16
I32 = jnp.int32

EPS = 1e-6
GRID_W = 64
RET_HEADS = 8
RET_HD = 256
ROPE_BASE = 10000.0
LRU_HEADS = 8
LRU_C = 8.0
CONV_W = 4
CONV_LEFT = 2
N_EXPERTS = 16
EC_CAPACITY = 2

LANES = 128
SUBLANES = 8
VMEM_LIMIT_BYTES = 60 * 1024 * 1024


def _params(*sem):
    return pltpu.CompilerParams(dimension_semantics=sem, vmem_limit_bytes=VMEM_LIMIT_BYTES)


def _rms(x):
    return x * lax.rsqrt(jnp.mean(x * x, axis=-1, keepdims=True) + EPS)


def _sigmoid(x):
    return 0.5 * jnp.tanh(0.5 * x) + 0.5


def _gelu_tanh(x):
    return 0.5 * x * (1.0 + jnp.tanh(0.7978845608028654 * (x + 0.044715 * (x * x * x))))


def _ada_kernel(c_ref, w_ref, b_ref, o_ref):
    c = c_ref[...]
    s = (c * _sigmoid(c)).astype(BF16)
    o_ref[...] = jnp.dot(s, w_ref[...].astype(BF16), preferred_element_type=F32) + b_ref[...]


def _ada_mod(cvecs, ada_w, ada_b, tn=512):
    d, w = ada_w.shape
    tn = min(tn, w)
    return pl.pallas_call(
        _ada_kernel,
        grid=(w // tn,),
        in_specs=[
            pl.BlockSpec((SUBLANES, d), lambda j: (0, 0)),
            pl.BlockSpec((d, tn), lambda j: (0, j)),
            pl.BlockSpec((1, tn), lambda j: (0, j)),
        ],
        out_specs=pl.BlockSpec((SUBLANES, tn), lambda j: (0, j)),
        out_shape=jax.ShapeDtypeStruct((SUBLANES, w), F32),
        compiler_params=_params("parallel"),
        name="ada_mod",
    )(cvecs, ada_w, ada_b.reshape(1, w))


def _prenorm_kernel(x_ref, g_ref, sc_ref, sh_ref, o_ref):
    y = _rms(x_ref[0]) * g_ref[...]
    o_ref[0] = (y * (1.0 + sc_ref[0]) + sh_ref[0]).astype(o_ref.dtype)


def _prenorm(x, g, scale, shift, tr=256):
    b, n, d = x.shape
    tr = min(tr, n)
    return pl.pallas_call(
        _prenorm_kernel,
        grid=(b, n // tr),
        in_specs=[
            pl.BlockSpec((1, tr, d), lambda i, j: (i, j, 0)),
            pl.BlockSpec((1, d), lambda i, j: (0, 0)),
            pl.BlockSpec((1, 1, d), lambda i, j: (i, 0, 0)),
            pl.BlockSpec((1, 1, d), lambda i, j: (i, 0, 0)),
        ],
        out_specs=pl.BlockSpec((1, tr, d), lambda i, j: (i, j, 0)),
        out_shape=jax.ShapeDtypeStruct((b, n, d), BF16),
        compiler_params=_params("parallel", "parallel"),
        name="prenorm",
    )(x, g.reshape(1, d), scale.reshape(b, 1, d), shift.reshape(b, 1, d))


def _proj_kernel(*refs, epilogue, scale):
    if epilogue == "rope":
        h_ref, w_ref, cos_ref, sin_ref, o_ref, wbf = refs
    else:
        h_ref, w_ref, o_ref, wbf = refs

    @pl.when(pl.program_id(1) == 0)
    def _():
        wbf[...] = w_ref[...].astype(BF16)

    acc = jnp.dot(h_ref[...], wbf[...], preferred_element_type=F32)
    if epilogue == "rope":
        for s in range(acc.shape[1] // LANES):
            a = acc[:, s * LANES:(s + 1) * LANES]
            ts = (s * LANES) % RET_HD
            cos = cos_ref[:, ts:ts + LANES]
            sin = sin_ref[:, ts:ts + LANES]
            o_ref[:, s * LANES:(s + 1) * LANES] = (
                a * cos + pltpu.roll(a, LANES // 2, 1) * sin).astype(o_ref.dtype)
        return
    if epilogue == "silu":
        acc = acc * _sigmoid(acc)
    elif epilogue == "gelu":
        acc = _gelu_tanh(acc)
    elif epilogue == "sigmoid":
        acc = _sigmoid(acc)
    elif epilogue == "scale":
        acc = acc * scale
    o_ref[...] = acc.astype(o_ref.dtype)


def _proj(h, w, col_off, width, epilogue="none", out_dtype=BF16, tables=None, rows_per_seq=None,
          scale=1.0, tm=1024, tn=512):
    m, k = h.shape
    tm = min(tm, m)
    tn = min(tn, width)
    if rows_per_seq is not None:
        tm = min(tm, rows_per_seq)
    jo = col_off // tn
    in_specs = [
        pl.BlockSpec((tm, k), lambda j, i: (i, 0)),
        pl.BlockSpec((k, tn), lambda j, i: (0, jo + j)),
    ]
    args = [h, w]
    if epilogue == "rope":
        nblk = rows_per_seq // tm
        tspec = pl.BlockSpec((tm, RET_HD), lambda j, i: (i % nblk, 0))
        in_specs += [tspec, tspec]
        args += list(tables)
    return pl.pallas_call(
        functools.partial(_proj_kernel, epilogue=epilogue, scale=scale),
        grid=(width // tn, m // tm),
        in_specs=in_specs,
        out_specs=pl.BlockSpec((tm, tn), lambda j, i: (i, j)),
        out_shape=jax.ShapeDtypeStruct((m, width), out_dtype),
        scratch_shapes=[pltpu.VMEM((k, tn), BF16)],
        compiler_params=_params("parallel", "arbitrary"),
        name="proj_" + epilogue,
    )(*args)


def _rope_tables(n, k_scale):
    quarter = RET_HD // 4
    freqs = ROPE_BASE ** (-jnp.arange(quarter, dtype=F32) / quarter)
    rows = n // GRID_W
    row = jnp.repeat(jnp.arange(rows, dtype=F32), GRID_W)
    col = jnp.tile(jnp.arange(GRID_W, dtype=F32), rows)

    def part(pos):
        ang = pos[:, None] * freqs[None, :]
        c, s = jnp.cos(ang), jnp.sin(ang)
        return jnp.concatenate([c, c], axis=-1), jnp.concatenate([-s, s], axis=-1)

    cr, sr = part(row)
    cc, sc = part(col)
    cos = jnp.concatenate([cr, cc], axis=-1)
    sin = jnp.concatenate([sr, sc], axis=-1)
    return (cos, sin), (cos * k_scale, sin * k_scale)


def _dot_nt(a, b):
    return lax.dot_general(a, b, (((1,), (1,)), ((), ())), preferred_element_type=F32)


def _dot_tn(a, b):
    return lax.dot_general(a, b, (((0,), (0,)), ((), ())), preferred_element_type=F32)


def _ret_kernel(lg_ref, qf_ref, kf_ref, vf_ref, qb_ref, kb_ref, vb_ref, kc_ref, vc_ref,
                of_ref, ob_ref, sf_ref, sb_ref, *, csz):
    h = pl.program_id(1)
    c = pl.program_id(2)
    lgf = lg_ref[0, h]
    lgb = lg_ref[1, h]
    nsub = qf_ref.shape[1] // csz

    @pl.when(c == 0)
    def _():
        kc = kc_ref[0].astype(F32)
        vc = vc_ref[0]
        lc = kc.shape[0]
        m = lax.broadcasted_iota(I32, (lc, 1), 0).astype(F32)
        sf_ref[...] = _dot_tn((kc * jnp.exp((lc - 1.0 - m) * lgf)).astype(BF16), vc)
        sb_ref[...] = _dot_tn((kc * jnp.exp(m * lgb)).astype(BF16), vc)

    qi = lax.broadcasted_iota(I32, (csz, csz), 0)
    ki = lax.broadcasted_iota(I32, (csz, csz), 1)
    j = lax.broadcasted_iota(I32, (csz, 1), 0).astype(F32)

    def direction(q_ref, k_ref, v_ref, o_ref, s_ref, lg, dist, q_pow, k_pow, order):
        intra = jnp.where(dist >= 0, jnp.exp(jnp.maximum(dist, 0).astype(F32) * lg), 0.0)
        q_dec = jnp.exp(q_pow * lg)
        k_dec = jnp.exp(k_pow * lg)
        chunk_dec = jnp.exp(jnp.full((1, 1), csz, F32) * lg)
        state = s_ref[...]
        for sub in order:
            rows = slice(sub * csz, (sub + 1) * csz)
            q = q_ref[0, rows, :]
            k = k_ref[0, rows, :]
            v = v_ref[0, rows, :]
            s = _dot_nt(q, k) * intra
            o = jnp.dot(s.astype(BF16), v, preferred_element_type=F32)
            o_ref[0, rows, :] = o + jnp.dot(q, state.astype(BF16), preferred_element_type=F32) * q_dec
            kd = (k.astype(F32) * k_dec).astype(BF16)
            state = state * chunk_dec + _dot_tn(kd, v)
        s_ref[...] = state

    direction(qf_ref, kf_ref, vf_ref, of_ref, sf_ref, lgf, qi - ki, j + 1.0, csz - 1.0 - j, range(nsub))
    direction(qb_ref, kb_ref, vb_ref, ob_ref, sb_ref, lgb, ki - qi, csz - j, j, range(nsub - 1, -1, -1))


def _retention(q, k, v, kc, vc, log_gamma, chunk=256, block=1024):
    b, n, hw = q.shape
    dh = RET_HD
    nh = hw // dh
    lc = kc.shape[1]
    chunk = min(chunk, n)
    block = min(block, n)
    nc = n // block
    fwd = pl.BlockSpec((1, block, dh), lambda bi, hi, ci: (bi, ci, hi))
    bwd = pl.BlockSpec((1, block, dh), lambda bi, hi, ci: (bi, nc - 1 - ci, hi))
    ctx = pl.BlockSpec((1, lc, dh), lambda bi, hi, ci: (bi, 0, hi))
    return pl.pallas_call(
        functools.partial(_ret_kernel, csz=chunk),
        grid=(b, nh, nc),
        in_specs=[pl.BlockSpec(memory_space=pltpu.SMEM), fwd, fwd, fwd, bwd, bwd, bwd, ctx, ctx],
        out_specs=[fwd, bwd],
        out_shape=[jax.ShapeDtypeStruct((b, n, hw), F32)] * 2,
        scratch_shapes=[pltpu.VMEM((dh, dh), F32), pltpu.VMEM((dh, dh), F32)],
        compiler_params=_params("parallel", "parallel", "arbitrary"),
        name="retention",
    )(log_gamma.astype(F32), q, k, v, q, k, v, kc, vc)


HALO = SUBLANES
SCAN_UNROLL = 8


def _lru_kernel(mf_ref, pf_ref, nf_ref, mb_ref, pb_ref, nb_ref, cw_ref, cb_ref, wg_ref, bg_ref, lam_ref,
                h0f_ref, h0b_ref, hf_ref, hb_ref, ext_s, a_s, b_s, cf_s, cb_s):
    t = pl.program_id(2)
    nt = pl.num_programs(2)
    tt = mf_ref.shape[1]
    w = mf_ref.shape[2]

    @pl.when(t == 0)
    def _():
        cf_s[...] = h0f_ref[0]
        cb_s[...] = h0b_ref[0]

    def coeffs(m_ref, p_ref, n_ref, has_prev, has_next, d):
        ext_s[0:HALO, :] = jnp.where(has_prev, p_ref[0], 0.0)
        ext_s[HALO:HALO + tt, :] = m_ref[0]
        ext_s[HALO + tt:, :] = jnp.where(has_next, n_ref[0], 0.0)
        u = cb_ref[...]
        for i in range(CONV_W):
            off = HALO + i - CONV_LEFT
            u = u + ext_s[off:off + tt, :] * cw_ref[i:i + 1, :]
        g = jnp.dot(u.astype(BF16), wg_ref[d, 0], preferred_element_type=F32) + bg_ref[d:d + 1, :]
        r = _sigmoid(g[:, :w])
        ig = _sigmoid(g[:, w:])
        lam = lam_ref[d:d + 1, :]
        softplus = jnp.maximum(-lam, 0.0) + jnp.log(1.0 + jnp.exp(-jnp.abs(lam)))
        log_a = -LRU_C * r * softplus
        a = jnp.exp(log_a)
        a_s[...] = a
        b_s[...] = jnp.sqrt(-jnp.tanh(log_a) * (a * a + 1.0)) * (ig * u)

    row = lax.broadcasted_iota(I32, (SUBLANES, w), 0)
    ntile = tt // SUBLANES

    coeffs(mf_ref, pf_ref, nf_ref, t > 0, t < nt - 1, 0)

    def fwd_body(i, carry):
        r0 = pl.multiple_of(i * SUBLANES, SUBLANES)
        a = a_s[pl.ds(r0, SUBLANES), :]
        b = b_s[pl.ds(r0, SUBLANES), :]
        for s in (1, 2, 4):
            keep = row >= s
            b = jnp.where(keep, a * pltpu.roll(b, s, 0) + b, b)
            a = jnp.where(keep, a * pltpu.roll(a, s, 0), a)
        hcur = a * carry + b
        hf_ref[0, pl.ds(r0, SUBLANES), :] = hcur
        return hcur[SUBLANES - 1:SUBLANES, :]

    cf_s[...] = lax.fori_loop(0, ntile, fwd_body, cf_s[...], unroll=min(SCAN_UNROLL, ntile))

    coeffs(mb_ref, pb_ref, nb_ref, t < nt - 1, t > 0, 1)

    def bwd_body(i, carry):
        r0 = pl.multiple_of((ntile - 1 - i) * SUBLANES, SUBLANES)
        a = a_s[pl.ds(r0, SUBLANES), :]
        b = b_s[pl.ds(r0, SUBLANES), :]
        for s in (1, 2, 4):
            keep = row < SUBLANES - s
            b = jnp.where(keep, a * pltpu.roll(b, SUBLANES - s, 0) + b, b)
            a = jnp.where(keep, a * pltpu.roll(a, SUBLANES - s, 0), a)
        hcur = a * carry + b
        hb_ref[0, pl.ds(r0, SUBLANES), :] = hcur
        return hcur[0:1, :]

    cb_s[...] = lax.fori_loop(0, ntile, bwd_body, cb_s[...], unroll=min(SCAN_UNROLL, ntile))


def _lru(lx, conv_w, conv_b, wg, bg, lam, h0f, h0b, tt=1024):
    b, n, wtot = lx.shape
    nh = wg.shape[1]
    hd = wtot // nh
    tt = min(tt, n)
    nt = n // tt
    hb_per = tt // HALO
    nhalo = n // HALO

    def main_spec(rev):
        return pl.BlockSpec((1, tt, hd), lambda bi, hi, ti: (bi, (nt - 1 - ti) if rev else ti, hi))

    def prev_spec(rev):
        return pl.BlockSpec(
            (1, HALO, hd),
            lambda bi, hi, ti: (bi, jnp.maximum(((nt - 1 - ti) if rev else ti) * hb_per - 1, 0), hi))

    def next_spec(rev):
        return pl.BlockSpec(
            (1, HALO, hd),
            lambda bi, hi, ti: (bi, jnp.minimum((((nt - 1 - ti) if rev else ti) + 1) * hb_per, nhalo - 1), hi))

    head_vec = lambda rows: pl.BlockSpec((rows, hd), lambda bi, hi, ti: (0, hi))
    state = pl.BlockSpec((1, 1, hd), lambda bi, hi, ti: (bi, 0, hi))
    bg2 = bg.reshape(2, nh * 2 * hd)
    return pl.pallas_call(
        _lru_kernel,
        grid=(b, nh, nt),
        in_specs=[
            main_spec(False), prev_spec(False), next_spec(False),
            main_spec(True), prev_spec(True), next_spec(True),
            head_vec(CONV_W), head_vec(1),
            pl.BlockSpec((2, 1, hd, 2 * hd), lambda bi, hi, ti: (0, hi, 0, 0)),
            pl.BlockSpec((2, 2 * hd), lambda bi, hi, ti: (0, hi)),
            head_vec(2), state, state,
        ],
        out_specs=[main_spec(False), main_spec(True)],
        out_shape=[jax.ShapeDtypeStruct((b, n, wtot), F32)] * 2,
        scratch_shapes=[pltpu.VMEM((tt + 2 * HALO, hd), F32), pltpu.VMEM((tt, hd), F32), pltpu.VMEM((tt, hd), F32),
                        pltpu.VMEM((1, hd), F32), pltpu.VMEM((1, hd), F32)],
        compiler_params=_params("parallel", "parallel", "arbitrary"),
        name="rglru",
    )(lx, lx, lx, lx, lx, lx, conv_w, conv_b.reshape(1, wtot), wg, bg2, lam,
      h0f.reshape(b, 1, wtot), h0b.reshape(b, 1, wtot))


def _mergeprep_kernel(of_ref, ob_ref, g_ref, hf_ref, hb_ref, ly_ref, o_ref):
    rw = of_ref.shape[1]
    for h in range(rw // RET_HD):
        sl = slice(h * RET_HD, (h + 1) * RET_HD)
        o = _rms(of_ref[:, sl] + ob_ref[:, sl])
        o_ref[:, sl] = (o * g_ref[:, sl].astype(F32)).astype(o_ref.dtype)
    o_ref[:, rw:] = (ly_ref[...].astype(F32) * (hf_ref[...] + hb_ref[...])).astype(o_ref.dtype)


def _mergeprep(of, ob, g, hf, hb, ly, tr=256):
    m, rw = of.shape
    lw = hf.shape[1]
    tr = min(tr, m)
    spec = lambda wd: pl.BlockSpec((tr, wd), lambda i: (i, 0))
    return pl.pallas_call(
        _mergeprep_kernel,
        grid=(m // tr,),
        in_specs=[spec(rw), spec(rw), spec(rw), spec(lw), spec(lw), spec(lw)],
        out_specs=spec(rw + lw),
        out_shape=jax.ShapeDtypeStruct((m, rw + lw), BF16),
        compiler_params=_params("parallel"),
        name="mergeprep",
    )(of, ob, g, hf, hb, ly)


def _merge_kernel(a_ref, wr_ref, wl_ref, gr_ref, gl_ref, o_ref, wr_bf, wl_bf):
    rw = wr_ref.shape[0]

    @pl.when(pl.program_id(1) == 0)
    def _():
        wr_bf[...] = wr_ref[...].astype(BF16)
        wl_bf[...] = wl_ref[...].astype(BF16)

    ret = jnp.dot(a_ref[:, :rw], wr_bf[...], preferred_element_type=F32)
    lru = jnp.dot(a_ref[:, rw:], wl_bf[...], preferred_element_type=F32)
    o_ref[...] = (gr_ref[...].astype(F32) * ret + gl_ref[...].astype(F32) * lru).astype(o_ref.dtype)


def _merge(a, w_ret_out, w_lru_out, gates, tm=1024, tn=512):
    m = a.shape[0]
    rw, d = w_ret_out.shape
    lw = w_lru_out.shape[0]
    tm = min(tm, m)
    tn = min(tn, d)
    nj = d // tn
    return pl.pallas_call(
        _merge_kernel,
        grid=(nj, m // tm),
        in_specs=[
            pl.BlockSpec((tm, rw + lw), lambda j, i: (i, 0)),
            pl.BlockSpec((rw, tn), lambda j, i: (0, j)),
            pl.BlockSpec((lw, tn), lambda j, i: (0, j)),
            pl.BlockSpec((tm, tn), lambda j, i: (i, j)),
            pl.BlockSpec((tm, tn), lambda j, i: (i, nj + j)),
        ],
        out_specs=pl.BlockSpec((tm, tn), lambda j, i: (i, j)),
        out_shape=jax.ShapeDtypeStruct((m, d), BF16),
        scratch_shapes=[pltpu.VMEM((rw, tn), BF16), pltpu.VMEM((lw, tn), BF16)],
        compiler_params=_params("parallel", "arbitrary"),
        name="merge",
    )(a, w_ret_out, w_lru_out, gates, gates)


def _post_mix_kernel(x_ref, mix_ref, gt_ref, g1_ref, g2_ref, sc_ref, sh_ref, rw_ref, x1_ref, h2_ref, lg_ref):
    x1 = x_ref[0] + gt_ref[0] * (_rms(mix_ref[0]) * g1_ref[...])
    x1_ref[0] = x1
    h2 = (_rms(x1) * g2_ref[...]) * (1.0 + sc_ref[0]) + sh_ref[0]
    h2_ref[0] = h2
    lg_ref[0] = _dot_nt(rw_ref[...], h2.astype(BF16))


def _post_mix(x, mix, gate, g_post, g_pre, scale, shift, router_w_t, tr=256):
    b, n, d = x.shape
    e = router_w_t.shape[0]
    tr = min(tr, n)
    rows = pl.BlockSpec((1, tr, d), lambda i, j: (i, j, 0))
    vec = pl.BlockSpec((1, d), lambda i, j: (0, 0))
    bvec = pl.BlockSpec((1, 1, d), lambda i, j: (i, 0, 0))
    return pl.pallas_call(
        _post_mix_kernel,
        grid=(b, n // tr),
        in_specs=[rows, rows, bvec, vec, vec, bvec, bvec, pl.BlockSpec((e, d), lambda i, j: (0, 0))],
        out_specs=[rows, rows, pl.BlockSpec((1, e, tr), lambda i, j: (i, 0, j))],
        out_shape=[jax.ShapeDtypeStruct((b, n, d), F32), jax.ShapeDtypeStruct((b, n, d), F32),
                   jax.ShapeDtypeStruct((b, e, n), F32)],
        compiler_params=_params("parallel", "parallel"),
        name="post_mix",
    )(x, mix, gate.reshape(b, 1, d), g_post.reshape(1, d), g_pre.reshape(1, d),
      scale.reshape(b, 1, d), shift.reshape(b, 1, d), router_w_t)


def _lane_cumsum_excl(m):
    e, n = m.shape
    tri = (lax.broadcasted_iota(I32, (LANES, LANES), 0) < lax.broadcasted_iota(I32, (LANES, LANES), 1)).astype(BF16)
    carry = jnp.zeros((e, 1), F32)
    outs = []
    for c in range(n // LANES):
        blk = m[:, c * LANES:(c + 1) * LANES]
        outs.append(jnp.dot(blk.astype(BF16), tri, preferred_element_type=F32) + carry)
        carry = carry + jnp.sum(blk, axis=1, keepdims=True)
    return jnp.concatenate(outs, axis=1)


SEL_SLOTS = 256
SEL_TOKENS = 512


def _sel_kernel(lg_ref, idx_ref, gate_ref, dst_ref, seg_ref, aff_s, pos_s, dst_s, starts_v, starts_sm, sem, *, cap):
    lg = lg_ref[0]
    ne, n = lg.shape
    ex = jnp.exp(lg - jnp.max(lg, axis=0, keepdims=True))
    aff = ex / jnp.sum(ex, axis=0, keepdims=True)
    thr_bits = jnp.zeros((ne, 1), I32)
    for bit in range(30, -1, -1):
        cand = thr_bits | (1 << bit)
        cnt = jnp.sum((aff >= lax.bitcast_convert_type(cand, F32)).astype(I32), axis=1, keepdims=True)
        thr_bits = jnp.where(cnt >= cap, cand, thr_bits)
    thr = lax.bitcast_convert_type(thr_bits, F32)

    above = aff > thr
    tied = (aff == thr).astype(F32)
    need = (cap - jnp.sum(above.astype(I32), axis=1, keepdims=True)).astype(F32)
    sel = jnp.where(above, 1.0, jnp.where(_lane_cumsum_excl(tied) < need, tied, 0.0))

    per_token = jnp.sum(sel, axis=0, keepdims=True)
    seg_lo = _lane_cumsum_excl(jnp.broadcast_to(per_token, (ne, n)))[0:1, :]
    seg_ref[0, 0:1, :] = seg_lo
    seg_ref[0, 1:2, :] = seg_lo + per_token
    taken = seg_lo
    for e in range(ne):
        dst_s[e:e + 1, :] = taken
        taken = taken + sel[e:e + 1, :]

    aff_s[...] = aff
    before = _lane_cumsum_excl(sel)
    pos_s[...] = (before + 1.0) * sel

    ss = min(SEL_SLOTS, cap)
    st = min(SEL_TOKENS, n)
    nchunk = n // st

    lane_e = lax.broadcasted_iota(I32, (ne, LANES), 1)
    starts = jnp.full((ne, LANES), cap, I32)
    for c in range(nchunk):
        starts = jnp.where(lane_e == c, before[:, c * st:c * st + 1].astype(I32), starts)
    starts_v[...] = starts
    to_smem = pltpu.make_async_copy(starts_v, starts_sm, sem)
    to_smem.start()
    to_smem.wait()

    lane = lax.broadcasted_iota(I32, (ss, LANES), 1)
    s_base = lax.broadcasted_iota(I32, (ss, st), 0).astype(F32) + 1.0
    t_base = lax.broadcasted_iota(I32, (1, st), 1).astype(F32)
    idx_ref[...] = jnp.zeros(idx_ref.shape, idx_ref.dtype)
    gate_ref[...] = jnp.zeros(gate_ref.shape, gate_ref.dtype)
    dst_ref[...] = jnp.zeros(dst_ref.shape, dst_ref.dtype)

    def fold(v):
        out = v[:, :LANES]
        for q in range(1, st // LANES):
            out = out + v[:, q * LANES:(q + 1) * LANES]
        return out

    def per_expert(e, _):
        for sb in range(cap // ss):
            sv = s_base + float(sb * ss)

            def per_chunk(c, carry):
                ia, ga, da = carry
                t0 = pl.multiple_of(c * st, st)
                hit = pos_s[pl.ds(e, 1), pl.ds(t0, st)] == sv
                tok = t_base + lax.convert_element_type(c * st, F32)
                ia = ia + fold(jnp.where(hit, tok, 0.0))
                ga = ga + fold(jnp.where(hit, aff_s[pl.ds(e, 1), pl.ds(t0, st)], 0.0))
                da = da + fold(jnp.where(hit, dst_s[pl.ds(e, 1), pl.ds(t0, st)], 0.0))
                return ia, ga, da

            c_lo = jnp.int32(0)
            c_hi = jnp.int32(0)
            for c in range(nchunk):
                c_lo = c_lo + (starts_sm[e, c + 1] <= sb * ss).astype(I32)
                c_hi = c_hi + (starts_sm[e, c] < (sb + 1) * ss).astype(I32)
            zero = jnp.zeros((ss, LANES), F32)
            ia, ga, da = lax.fori_loop(c_lo, c_hi, per_chunk, (zero, zero, zero))
            rows = slice(sb * ss, (sb + 1) * ss)
            mine = lane == e
            for ref, acc in ((idx_ref, ia), (gate_ref, ga), (dst_ref, da)):
                col = jnp.sum(acc, axis=1, keepdims=True).astype(ref.dtype)
                ref[0, rows, :] = jnp.where(mine, col, ref[0, rows, :])
        return 0

    lax.fori_loop(0, ne, per_expert, 0)


def _expert_choice(logits, cap):
    b, ne, n = logits.shape
    slot = pl.BlockSpec((1, cap, LANES), lambda i: (i, 0, 0))
    return pl.pallas_call(
        functools.partial(_sel_kernel, cap=cap),
        grid=(b,),
        in_specs=[pl.BlockSpec((1, ne, n), lambda i: (i, 0, 0))],
        out_specs=[slot, slot, slot, pl.BlockSpec((1, 2, n), lambda i: (i, 0, 0))],
        out_shape=[jax.ShapeDtypeStruct((b, cap, LANES), I32), jax.ShapeDtypeStruct((b, cap, LANES), F32),
                   jax.ShapeDtypeStruct((b, cap, LANES), I32), jax.ShapeDtypeStruct((b, 2, n), F32)],
        scratch_shapes=[pltpu.VMEM((ne, n), F32)] * 3 + [pltpu.VMEM((ne, LANES), I32), pltpu.SMEM((ne, LANES), I32),
                                                       pltpu.SemaphoreType.DMA],
        compiler_params=_params("arbitrary"),
        name="expert_choice",
    )(logits)


def _row_copy(src, dst, src_row, dst_row, sem):
    return pltpu.make_async_copy(src.at[pl.ds(src_row, 1), :], dst.at[pl.ds(dst_row, 1), :], sem)


def _ffn_kernel(src_ref, dst_ref, h_hbm, w1_ref, w3_ref, w2_ref, gate_ref, y_hbm, xin, xbf, acc, gsem, psem, *, nf):
    tm, d = xbf.shape
    per_step = tm // nf
    i = pl.program_id(0)
    f = pl.program_id(1)
    nt = pl.num_programs(0)
    cur = i % 2
    oth = 1 - cur

    def gather(tile, r):
        return _row_copy(h_hbm, xin, src_ref[tile * tm + r], r, gsem)

    def put(tile, slot, r):
        return _row_copy(acc.at[slot], y_hbm, r, dst_ref[tile * tm + r], psem.at[slot])

    def all_gathered():
        pltpu.make_async_copy(h_hbm.at[pl.ds(0, tm), :], xin, gsem).wait()

    def all_put(slot):
        pltpu.make_async_copy(acc.at[slot], y_hbm.at[pl.ds(0, tm), :], psem.at[slot]).wait()

    def for_rows(fn):
        def body(r, carry):
            fn(r)
            return carry
        lax.fori_loop(0, tm, body, 0)

    @pl.when(jnp.logical_and(i == 0, f == 0))
    def _():
        for_rows(lambda r: gather(0, r).start())

    @pl.when(f == 0)
    def _():
        all_gathered()
        xbf[...] = xin[...].astype(BF16)
        acc[cur] = jnp.zeros((tm, d), F32)

    nxt = jnp.where(i + 1 < nt, i + 1, 0)

    def step(with_puts):
        for r in range(per_step):
            gather(nxt, f * per_step + r).start()
        if with_puts:
            for r in range(per_step):
                put(i - 1, oth, f * per_step + r).start()
        x = xbf[...]
        h1 = jnp.dot(x, w1_ref[0].astype(BF16), preferred_element_type=F32)
        h3 = jnp.dot(x, w3_ref[0].astype(BF16), preferred_element_type=F32)
        hid = (h1 * _sigmoid(h1) * h3).astype(BF16)
        acc[cur] += jnp.dot(hid, w2_ref[0].astype(BF16), preferred_element_type=F32)

    lax.cond(i > 0, lambda: step(True), lambda: step(False))

    @pl.when(f == nf - 1)
    def _():
        acc[cur] = acc[cur] * gate_ref[...]

        @pl.when(i > 0)
        def _():
            all_put(oth)

        @pl.when(i == nt - 1)
        def _():
            for_rows(lambda r: put(i, cur, r).start())
            all_put(cur)
            all_gathered()


def _expert_ffn(h2, w1, w3, w2, src_rows, dst_rows, gate, tm=512, tf=256):
    ne, d, ff = w1.shape
    total = src_rows.shape[0]
    s = total // ne
    tm = min(tm, s)
    tf = min(tf, ff)
    nf = ff // tf
    per_e = s // tm
    assert tm % nf == 0, "row copies are issued in equal shares per grid step"
    return pl.pallas_call(
        functools.partial(_ffn_kernel, nf=nf),
        grid_spec=pltpu.PrefetchScalarGridSpec(
            num_scalar_prefetch=2,
            grid=(ne * per_e, nf),
            in_specs=[
                pl.BlockSpec(memory_space=pl.ANY),
                pl.BlockSpec((1, d, tf), lambda i, f, sr, dr: (i // per_e, 0, f)),
                pl.BlockSpec((1, d, tf), lambda i, f, sr, dr: (i // per_e, 0, f)),
                pl.BlockSpec((1, tf, d), lambda i, f, sr, dr: (i // per_e, f, 0)),
                pl.BlockSpec((tm, 1), lambda i, f, sr, dr: (i, 0)),
            ],
            out_specs=pl.BlockSpec(memory_space=pl.ANY),
            scratch_shapes=[pltpu.VMEM((tm, d), F32), pltpu.VMEM((tm, d), BF16), pltpu.VMEM((2, tm, d), F32),
                            pltpu.SemaphoreType.DMA, pltpu.SemaphoreType.DMA((2,))],
        ),
        out_shape=jax.ShapeDtypeStruct((total, d), F32),
        compiler_params=_params("arbitrary", "arbitrary"),
        name="expert_ffn",
    )(src_rows, dst_rows, h2, w1, w3, w2, gate)


SEG_ROWS = 256


def _combine_kernel(first_ref, last_ref, x_ref, lo_ref, hi_ref, gt_ref, g_ref, y_hbm, o_ref, buf, acc, sem, slot0, *,
                    rows_per_sample):
    ntile = pl.num_programs(0) * pl.num_programs(1)
    tile = pl.program_id(0) * pl.num_programs(1) + pl.program_id(1)

    def extent(t):
        first = first_ref[t]
        last = last_ref[t]
        r0 = (first // SUBLANES) * SUBLANES
        return r0, jnp.where(last > first, (last - r0 + SEG_ROWS - 1) // SEG_ROWS, 0)

    def chunk_start(r0, c):
        return pl.multiple_of(jnp.minimum(r0 + c * SEG_ROWS, rows_per_sample - SEG_ROWS), SUBLANES)

    def chunk_copy(t, r0, c, slot):
        sample = t // pl.num_programs(1)
        src = pl.multiple_of(sample * rows_per_sample + chunk_start(r0, c), SUBLANES)
        return pltpu.make_async_copy(y_hbm.at[pl.ds(src, SEG_ROWS), :], buf.at[slot], sem.at[slot])

    r0, nch = extent(tile)

    @pl.when(tile == 0)
    def _():
        slot0[0] = 0

        @pl.when(nch > 0)
        def _():
            chunk_copy(tile, r0, 0, 0).start()

    base = slot0[0]
    acc[...] = jnp.zeros(acc.shape, acc.dtype)
    lo = lo_ref[0]
    hi = hi_ref[0]

    def body(c, carry):
        slot = (base + c) % 2

        @pl.when(c + 1 < nch)
        def _():
            chunk_copy(tile, r0, c + 1, 1 - slot).start()

        chunk_copy(tile, r0, c, slot).wait()
        row = (lax.broadcasted_iota(I32, (1, SEG_ROWS), 1) + chunk_start(r0, c)).astype(F32)
        nominal = (r0 + c * SEG_ROWS).astype(F32)
        pick = jnp.logical_and(jnp.logical_and(lo <= row, row < hi), row >= nominal)
        pick = jnp.where(pick, 1.0, 0.0).astype(BF16)
        acc[...] += jnp.dot(pick, buf[slot].astype(BF16), preferred_element_type=F32)
        return carry

    lax.fori_loop(0, nch, body, 0)

    after = (base + nch) % 2
    slot0[0] = after

    @pl.when(tile + 1 < ntile)
    def _():
        nxt = jnp.minimum(tile + 1, ntile - 1)
        r0n, nchn = extent(nxt)

        @pl.when(nchn > 0)
        def _():
            chunk_copy(nxt, r0n, 0, after).start()

    o_ref[0] = x_ref[0] + gt_ref[0] * (_rms(acc[...]) * g_ref[...])


def _combine(x1, y, seg, gate, g, tt=512):
    b, n, d = x1.shape
    rows_per_sample = y.shape[0] // b
    tt = min(tt, n)
    nj = n // tt
    seg_i = seg.astype(I32)
    first = seg_i[:, 0, ::tt].reshape(-1)
    last = seg_i[:, 1, tt - 1::tt].reshape(-1)
    rows = pl.BlockSpec((1, tt, d), lambda i, j, fr, lr: (i, j, 0))
    col = pl.BlockSpec((1, tt, 1), lambda i, j, fr, lr: (i, j, 0))
    return pl.pallas_call(
        functools.partial(_combine_kernel, rows_per_sample=rows_per_sample),
        grid_spec=pltpu.PrefetchScalarGridSpec(
            num_scalar_prefetch=2,
            grid=(b, nj),
            in_specs=[rows, col, col,
                      pl.BlockSpec((1, 1, d), lambda i, j, fr, lr: (i, 0, 0)),
                      pl.BlockSpec((1, d), lambda i, j, fr, lr: (0, 0)),
                      pl.BlockSpec(memory_space=pl.ANY)],
            out_specs=rows,
            scratch_shapes=[pltpu.VMEM((2, SEG_ROWS, d), F32), pltpu.VMEM((tt, d), F32),
                            pltpu.SemaphoreType.DMA((2,)), pltpu.SMEM((1,), I32)],
        ),
        out_shape=jax.ShapeDtypeStruct((b, n, d), F32),
        compiler_params=_params("arbitrary", "arbitrary"),
        name="combine_residual",
    )(first, last, x1, seg[:, 0, :, None], seg[:, 1, :, None], gate.reshape(b, 1, d), g.reshape(1, d), y)


def _layer(x, c, ctx, c_ctx, p):
    b, n, d = x.shape
    lc = ctx.shape[1]
    rw = RET_HEADS * RET_HD
    lw = p["conv_w"].shape[1]
    q_off, k_off, v_off, g_off = 0, rw, 2 * rw, 3 * rw
    lx_off = 4 * rw
    ly_off = lx_off + lw
    gr_off = ly_off + lw

    cvecs = jnp.zeros((SUBLANES, d), F32).at[:b].set(c).at[b].set(c_ctx)
    mod = _ada_mod(cvecs, p["ada_w"], p["ada_b"])
    sh1, sc1, gt1, sh2, sc2, gt2 = [mod[:b, i * d:(i + 1) * d] for i in range(6)]
    csh1 = jnp.broadcast_to(mod[b:b + 1, :d], (b, d))
    csc1 = jnp.broadcast_to(mod[b:b + 1, d:2 * d], (b, d))

    w_in = p["w_in"]
    h = _prenorm(x, p["n_pre_mix"], sc1, sh1).reshape(b * n, d)
    hc = _prenorm(ctx, p["n_pre_mix"], csc1, csh1).reshape(b * lc, d)

    k_scale = RET_HD ** -0.5
    q_tabs, k_tabs = _rope_tables(n, k_scale)
    q = _proj(h, w_in, q_off, rw, "rope", tables=q_tabs, rows_per_seq=n)
    k = _proj(h, w_in, k_off, rw, "rope", tables=k_tabs, rows_per_seq=n)
    v = _proj(h, w_in, v_off, rw)
    g = _proj(h, w_in, g_off, rw, "silu")
    lx = _proj(h, w_in, lx_off, lw, out_dtype=F32)
    ly = _proj(h, w_in, ly_off, lw, "gelu")
    gates = _proj(h, w_in, gr_off, 2 * d, "sigmoid")
    kc = _proj(hc, w_in, k_off, rw, "scale", scale=k_scale)
    vc = _proj(hc, w_in, v_off, rw)
    lxc = _proj(hc, w_in, lx_off, lw, out_dtype=F32)

    of, ob = _retention(q.reshape(b, n, rw), k.reshape(b, n, rw), v.reshape(b, n, rw),
                        kc.reshape(b, lc, rw), vc.reshape(b, lc, rw), p["ret_lg"])

    wg = jnp.concatenate([p["gate_a_w"], p["gate_x_w"]], axis=-1).astype(BF16)
    hd = lw // LRU_HEADS
    bg = jnp.concatenate([p["gate_a_b"].reshape(2, LRU_HEADS, hd), p["gate_x_b"].reshape(2, LRU_HEADS, hd)], axis=-1)
    lru_w = (p["conv_w"], p["conv_b"], wg, bg, p["lam"])
    zero_state = jnp.zeros((b, lw), F32)
    hfc, hbc = _lru(lxc.reshape(b, lc, lw), *lru_w, zero_state, zero_state)
    hf, hb = _lru(lx.reshape(b, n, lw), *lru_w, hfc[:, -1], hbc[:, 0])

    a = _mergeprep(of.reshape(b * n, rw), ob.reshape(b * n, rw), g,
                   hf.reshape(b * n, lw), hb.reshape(b * n, lw), ly)
    merged = _merge(a, p["w_ret_out"], p["w_lru_out"], gates)
    mix = _proj(merged, p["w_out"], 0, d, out_dtype=F32)
    x1, h2, logits = _post_mix(x, mix.reshape(b, n, d), gt1, p["n_post_mix"], p["n_pre_ffn"], sc2, sh2,
                               p["router_w"].T.astype(BF16))

    ne = p["router_w"].shape[1]
    cap = EC_CAPACITY * n // ne
    idx, gate, dst, seg = _expert_choice(logits, cap)
    by_expert = lambda a: jnp.transpose(a[:, :, :ne], (2, 0, 1))
    sample = jnp.arange(b, dtype=I32)[None, :, None]
    src_rows = (by_expert(idx) + sample * n).reshape(-1)
    dst_rows = (by_expert(dst) + sample * (ne * cap)).reshape(-1)
    y = _expert_ffn(h2.reshape(b * n, d), p["w1"], p["w3"], p["w2"],
                    src_rows, dst_rows, by_expert(gate).reshape(-1, 1))
    return _combine(x1, y, seg, gt2, p["n_post_ffn"])


def kernel(x, c, ctx, c_ctx, ada_w, ada_b, norm_pre_mix, norm_post_mix, norm_pre_ffn, norm_post_ffn, w_in, ret_log_gamma, w_ret_out, lru_conv_w, lru_conv_b, lru_gate_a_w, lru_gate_a_b, lru_gate_x_w, lru_gate_x_b, lru_lambda, w_lru_out, w_out, router_w, expert_w1, expert_w3, expert_w2):
    depth = ada_w.shape[0]
    assert depth == 1, "context-stream update between layers is not implemented"
    l = 0
    p = {
        "ada_w": ada_w[l], "ada_b": ada_b[l],
        "n_pre_mix": norm_pre_mix[l], "n_post_mix": norm_post_mix[l],
        "n_pre_ffn": norm_pre_ffn[l], "n_post_ffn": norm_post_ffn[l],
        "w_in": w_in[l], "ret_lg": ret_log_gamma[l], "w_ret_out": w_ret_out[l],
        "conv_w": lru_conv_w[l], "conv_b": lru_conv_b[l],
        "gate_a_w": lru_gate_a_w[l], "gate_a_b": lru_gate_a_b[l],
        "gate_x_w": lru_gate_x_w[l], "gate_x_b": lru_gate_x_b[l], "lam": lru_lambda[l],
        "w_lru_out": w_lru_out[l], "w_out": w_out[l], "router_w": router_w[l],
        "w1": expert_w1[l], "w3": expert_w3[l], "w2": expert_w2[l],
    }
    return _layer(x, c, ctx, c_ctx, p)
```
